```python
import math
import jax, jax.numpy as jnp
from jax import lax
import numpy as np

D_MODEL = 2048
BATCH = 4
SEQ = 2048
DEPTH = 2
DEC_BATCH = 32
DEC_SEQ = 64
PAST_LEN = 4096

CHUNK = 64
N_A_LAYERS = DEPTH // 2
N_B_LAYERS = DEPTH - N_A_LAYERS
HEAD_DIM = 128
D_MIX = D_MODEL
D_MEM = D_MIX // 4
N_MEM_HEADS = D_MEM // HEAD_DIM
D_MAIN = D_MIX - D_MEM
N_SB_HEADS = D_MAIN // HEAD_DIM
CONV_W = 3
N_MEM = 256
D_FF = 4 * D_MODEL
Q_BLOCK = 128
K_BLOCK = 128
ALPHA = (2.0 * DEPTH) ** 0.25
BETA = (8.0 * DEPTH) ** -0.25
LN_EPS = 1e-5

kernel_name = 'yoco_shortconv_stickbreak_step'


def layer_norm(x, g, b):
    xf = x.astype(jnp.float32)
    mu = jnp.mean(xf, axis=-1, keepdims=True)
    var = jnp.mean(jnp.square(xf - mu), axis=-1, keepdims=True)
    y = (xf - mu) * lax.rsqrt(var + LN_EPS) * g.astype(jnp.float32) + b.astype(jnp.float32)
    return y.astype(x.dtype)


def short_conv(u, w, prev):
    T = u.shape[1]
    up = jnp.concatenate([prev, u], axis=1)
    y = up[:, 0:T] * w[:, 0]
    for j in range(1, CONV_W):
        y = y + up[:, j:j + T] * w[:, j]
    return y, up[:, -(CONV_W - 1):]


def mem_attend(q, mk, mv):
    s = jnp.einsum('bthd,bmhd->bhtm', q, mk).astype(jnp.float32) / math.sqrt(HEAD_DIM)
    p = jax.nn.softmax(s, axis=-1).astype(mv.dtype)
    return jnp.einsum('bhtm,bmhd->bthd', p, mv)


def stick_breaking(q, k, v, q_start):
    B, Tq, H, Dh = q.shape
    Tk = k.shape[1]
    qb = Q_BLOCK if Tq % Q_BLOCK == 0 else Tq
    nq = Tq // qb
    nk = -(-Tk // K_BLOCK)
    pad = nk * K_BLOCK - Tk
    k = jnp.pad(k, ((0, 0), (0, pad), (0, 0), (0, 0)))
    v = jnp.pad(v, ((0, 0), (0, pad), (0, 0), (0, 0)))
    kb = k.reshape(B, nk, K_BLOCK, H, Dh).transpose(1, 0, 2, 3, 4)
    vb = v.reshape(B, nk, K_BLOCK, H, Dh).transpose(1, 0, 2, 3, 4)
    k_pos = jnp.arange(nk * K_BLOCK, dtype=jnp.int32).reshape(nk, K_BLOCK)
    qs = q.reshape(B, nq, qb, H, Dh).transpose(1, 0, 2, 3, 4)
    q_pos = (q_start + jnp.arange(Tq, dtype=jnp.int32)).reshape(nq, qb)
    scale = 1.0 / math.sqrt(Dh)

    def q_block(args):
        qi, qp = args

        def k_step(carry, kargs):
            acc, out = carry
            ki, vi, kp = kargs
            z = jnp.einsum('bqhd,bkhd->bhqk', qi, ki).astype(jnp.float32) * scale
            valid = kp[None, :] < qp[:, None]
            u = jnp.where(valid, jax.nn.log_sigmoid(-z), 0.0)
            suffix = lax.cumsum(u, axis=3, reverse=True) - u
            log_a = jax.nn.log_sigmoid(z) + suffix + acc[..., None]
            a = jnp.where(valid, jnp.exp(log_a), 0.0)
            out = out + jnp.einsum('bhqk,bkhd->bhqd', a, vi.astype(jnp.float32))
            return (acc + jnp.sum(u, axis=-1), out), None

        init = (jnp.zeros((B, H, qb), jnp.float32), jnp.zeros((B, H, qb, Dh), jnp.float32))
        (_, out), _ = lax.scan(k_step, init, (kb, vb, k_pos), reverse=True)
        return out.transpose(0, 2, 1, 3)

    o = lax.map(q_block, (qs, q_pos))
    return o.transpose(1, 0, 2, 3, 4).reshape(B, Tq, H, Dh).astype(q.dtype)


def mem_kv_from_tokens(mem, w_mem_kv):
    kv = jnp.einsum('bmd,lde->lbme', mem, w_mem_kv)
    B = mem.shape[0]
    mk = kv[..., :D_MEM].reshape(DEPTH, B, N_MEM, N_MEM_HEADS, HEAD_DIM)
    mv = kv[..., D_MEM:].reshape(DEPTH, B, N_MEM, N_MEM_HEADS, HEAD_DIM)
    return mk, mv


def trunk(x, conv_prev, k_past, v_past, mem_k, mem_v, q_start,
          w_in_a, conv_w, w_in_b, w_kv, w_out, w_ff1, w_ff2, ln_g, ln_b):
    B, T, _ = x.shape
    conv_states = []
    k_new = None
    v_new = None
    k_all = None
    v_all = None
    for layer in range(DEPTH):
        if layer < N_A_LAYERS:
            h = x @ w_in_a[layer]
            xin = h[..., :D_MAIN]
            gate_b = h[..., D_MAIN:2 * D_MAIN]
            gate_c = h[..., 2 * D_MAIN:3 * D_MAIN]
            q_m = h[..., 3 * D_MAIN:]
            cu, st = short_conv(gate_c * xin, conv_w[layer], conv_prev[layer])
            conv_states.append(st)
            main = gate_b * cu
        else:
            if k_new is None:
                kv = x @ w_kv
                k_new = kv[..., :D_MAIN].reshape(B, T, N_SB_HEADS, HEAD_DIM)
                v_new = kv[..., D_MAIN:].reshape(B, T, N_SB_HEADS, HEAD_DIM)
                if k_past is None:
                    k_all, v_all = k_new, v_new
                else:
                    k_all = jnp.concatenate([k_past, k_new], axis=1)
                    v_all = jnp.concatenate([v_past, v_new], axis=1)
            h = x @ w_in_b[layer - N_A_LAYERS]
            q_sb = h[..., :D_MAIN].reshape(B, T, N_SB_HEADS, HEAD_DIM)
            q_m = h[..., D_MAIN:]
            main = stick_breaking(q_sb, k_all, v_all, q_start).reshape(B, T, D_MAIN)
        mo = mem_attend(q_m.reshape(B, T, N_MEM_HEADS, HEAD_DIM), mem_k[layer], mem_v[layer])
        mix = jnp.concatenate([main, mo.reshape(B, T, D_MEM)], axis=-1) @ w_out[layer]
        x = layer_norm(ALPHA * x + mix, ln_g[layer, 0], ln_b[layer, 0])
        f = jnp.square(jax.nn.relu(x @ w_ff1[layer])) @ w_ff2[layer]
        x = layer_norm(ALPHA * x + f, ln_g[layer, 1], ln_b[layer, 1])
    return x, jnp.stack(conv_states), k_new, v_new


def setup_inputs(seed: int = 0) -> dict:
    key = jax.random.key(seed)
    ks = jax.random.split(key, 24)

    def nrm(k, shape, s=1.0):
        return jax.random.normal(k, shape, jnp.float32) * s

    d_in_a = 3 * D_MAIN + D_MEM
    d_in_b = D_MAIN + D_MEM
    return {
        'x_prompt': nrm(ks[0], (BATCH, SEQ, D_MODEL)),
        'x_sample': nrm(ks[1], (DEC_BATCH, DEC_SEQ, D_MODEL)),
        'state_conv': nrm(ks[2], (N_A_LAYERS, DEC_BATCH, CONV_W - 1, D_MAIN)),
        'cache_k': nrm(ks[3], (DEC_BATCH, PAST_LEN, N_SB_HEADS, HEAD_DIM)),
        'cache_v': nrm(ks[4], (DEC_BATCH, PAST_LEN, N_SB_HEADS, HEAD_DIM)),
        'cache_mem_k': nrm(ks[5], (DEPTH, DEC_BATCH, N_MEM, N_MEM_HEADS, HEAD_DIM)),
        'cache_mem_v': nrm(ks[6], (DEPTH, DEC_BATCH, N_MEM, N_MEM_HEADS, HEAD_DIM)),
        'mem_prompt': nrm(ks[7], (BATCH, N_MEM, D_MODEL)),
        'w_in_a': nrm(ks[8], (N_A_LAYERS, D_MODEL, d_in_a), D_MODEL ** -0.5),
        'conv_w': nrm(ks[9], (N_A_LAYERS, D_MAIN, CONV_W), CONV_W ** -0.5),
        'w_in_b': nrm(ks[10], (N_B_LAYERS, D_MODEL, d_in_b), D_MODEL ** -0.5),
        'w_kv': jnp.concatenate([nrm(ks[11], (D_MODEL, D_MAIN), D_MODEL ** -0.5),
                                 nrm(ks[12], (D_MODEL, D_MAIN), D_MODEL ** -0.5 * BETA)], axis=-1),
        'w_mem_kv': jnp.concatenate([nrm(ks[13], (DEPTH, D_MODEL, D_MEM), D_MODEL ** -0.5),
                                     nrm(ks[14], (DEPTH, D_MODEL, D_MEM), D_MODEL ** -0.5 * BETA)], axis=-1),
        'w_out': nrm(ks[15], (DEPTH, D_MIX, D_MODEL), D_MIX ** -0.5 * BETA),
        'w_ff1': nrm(ks[16], (DEPTH, D_MODEL, D_FF), D_MODEL ** -0.5),
        'w_ff2': nrm(ks[17], (DEPTH, D_FF, D_MODEL), D_FF ** -0.5 * BETA),
        'ln_g': 1.0 + nrm(ks[18], (DEPTH, 2, D_MODEL), 0.01),
        'ln_b': nrm(ks[19], (DEPTH, 2, D_MODEL), 0.01),
    }


def reference(x_prompt, x_sample, state_conv, cache_k, cache_v, cache_mem_k, cache_mem_v,
              mem_prompt, w_in_a, conv_w, w_in_b, w_kv, w_mem_kv, w_out, w_ff1, w_ff2,
              ln_g, ln_b):
    mem_k_p, mem_v_p = mem_kv_from_tokens(mem_prompt, w_mem_kv)
    conv_zero = jnp.zeros((N_A_LAYERS, x_prompt.shape[0], CONV_W - 1, D_MAIN), x_prompt.dtype)
    y_prompt, conv_p, k_p, v_p = trunk(
        x_prompt, conv_zero, None, None, mem_k_p, mem_v_p, 0,
        w_in_a, conv_w, w_in_b, w_kv, w_out, w_ff1, w_ff2, ln_g, ln_b)
    y_sample, conv_s, k_s, v_s = trunk(
        x_sample, state_conv, cache_k, cache_v, cache_mem_k, cache_mem_v, cache_k.shape[1],
        w_in_a, conv_w, w_in_b, w_kv, w_out, w_ff1, w_ff2, ln_g, ln_b)
    return (y_prompt, y_sample, conv_p, conv_s, k_p, v_p, k_s, v_s, mem_k_p, mem_v_p)
```

```python
import functools
import math

import jax
import jax.numpy as jnp
import numpy as np
from jax import lax
from jax.experimental import pallas as pl
from jax.experimental.pallas import tpu as pltpu

F32 = jnp.float32
BF16 = jnp.bfloat16

HEAD_DIM = 128
LN_EPS = 1e-5
K_BLOCK = 128
HALO_ROWS = 16
F32_EXP_UNDERFLOW = 104.0
VMEM_LIMIT_BYTES = 48 * 1024 * 1024

_NT = (((1,), (1,)), ((), ()))


def _dot(a, b):
    return jnp.dot(a, b, preferred_element_type=F32)


def _params(grid_rank):
    return pltpu.CompilerParams(dimension_semantics=("arbitrary",) * grid_rank,
                                vmem_limit_bytes=VMEM_LIMIT_BYTES)


def _layer_norm(y, g, b):
    mu = jnp.mean(y, axis=-1, keepdims=True)
    d = y - mu
    var = jnp.mean(d * d, axis=-1, keepdims=True)
    return d * lax.rsqrt(var + LN_EPS) * g + b


def _seq_tiling(n_seq, seq_len, rows):
    if seq_len >= rows:
        assert seq_len % rows == 0
        return 1, rows, seq_len // rows
    assert rows % seq_len == 0 and n_seq % (rows // seq_len) == 0
    return rows // seq_len, seq_len, 1


def _store_heads(ref, val):
    for h in range(ref.shape[1]):
        ref[:, h, :] = val[:, h * HEAD_DIM:(h + 1) * HEAD_DIM].astype(ref.dtype)


def _memkv_kernel(mem_ref, w_ref, k_ref, v_ref):
    r = _dot(mem_ref[...].astype(BF16), w_ref[...])
    d_mem = k_ref.shape[1] * HEAD_DIM
    _store_heads(k_ref, r[:, :d_mem])
    _store_heads(v_ref, r[:, d_mem:])


def _memkv(mem, w_mem_kv_b):
    rows, d = mem.shape
    depth, _, two_dmem = w_mem_kv_b.shape
    n_heads = two_dmem // 2 // HEAD_DIM
    out = jax.ShapeDtypeStruct((depth, rows, n_heads, HEAD_DIM), F32)
    return pl.pallas_call(
        _memkv_kernel,
        grid=(depth,),
        in_specs=[pl.BlockSpec((rows, d), lambda l: (0, 0)),
                  pl.BlockSpec((None, d, two_dmem), lambda l: (l, 0, 0))],
        out_specs=[pl.BlockSpec((None, rows, n_heads, HEAD_DIM), lambda l: (l, 0, 0, 0))] * 2,
        out_shape=[out, out],
        compiler_params=_params(1),
        name="memkv",
    )(mem, w_mem_kv_b)


def _inproj_a_kernel(x_ref, halo_ref, wx_ref, wb_ref, wc_ref, cw_ref, main_ref, st_ref, u_scr,
                     *, S, L, tiles_per_seq, has_state):
    x = x_ref[...].astype(BF16)
    wx = wx_ref[...]
    wc = wc_ref[...]
    xin = _dot(x, wx)
    gate_b = _dot(x, wb_ref[...])
    gate_c = _dot(x, wc)
    u = gate_c * xin
    cw = cw_ref[...]
    w0, w1, w2 = cw[0:1], cw[1:2], cw[2:3]
    if not has_state:
        xh = halo_ref[...].astype(BF16)
        uh = _dot(xh, wc) * _dot(xh, wx)
        first = (pl.program_id(0) % tiles_per_seq) == 0
        u_scr[0:HALO_ROWS, :] = jnp.where(first, 0.0, uh)
    for s in range(S):
        rows = slice(s * L, (s + 1) * L)
        if has_state:
            u_scr[HALO_ROWS - 2:HALO_ROWS, :] = halo_ref[s]
        u_scr[HALO_ROWS:HALO_ROWS + L, :] = u[rows]
        y = u_scr[HALO_ROWS - 2:HALO_ROWS - 2 + L, :] * w0
        y = y + u_scr[HALO_ROWS - 1:HALO_ROWS - 1 + L, :] * w1
        y = y + u[rows] * w2
        main_ref[rows, :] = (gate_b[rows] * y).astype(main_ref.dtype)
        st_ref[s] = u_scr[HALO_ROWS + L - 2:HALO_ROWS + L, :]


def _inproj_a(x, state, w_in_a_b, conv_w_t, layer, n_seq, seq_len, *, rows=512, tn=512):
    m, d = x.shape
    c = conv_w_t.shape[-1]
    S, L, tps = _seq_tiling(n_seq, seq_len, rows)
    nct = c // tn
    has_state = state is not None
    if has_state:
        assert tps == 1
        halo = state
        halo_spec = pl.BlockSpec((None, S, 2, tn), lambda i, j: (layer, i, 0, j))
    else:
        halo = x
        halo_spec = pl.BlockSpec(
            (HALO_ROWS, d), lambda i, j: (jnp.maximum(i * (rows // HALO_ROWS) - 1, 0), 0))
    kern = functools.partial(_inproj_a_kernel, S=S, L=L, tiles_per_seq=tps, has_state=has_state)
    w_spec = lambda off: pl.BlockSpec((None, d, tn), lambda i, j: (layer, 0, off + j))
    main, tile_state = pl.pallas_call(
        kern,
        grid=(m // rows, nct),
        in_specs=[pl.BlockSpec((rows, d), lambda i, j: (i, 0)),
                  halo_spec,
                  w_spec(0), w_spec(nct), w_spec(2 * nct),
                  pl.BlockSpec((None, 3, tn), lambda i, j: (layer, 0, j))],
        out_specs=[pl.BlockSpec((rows, tn), lambda i, j: (i, j)),
                   pl.BlockSpec((S, 2, tn), lambda i, j: (i, 0, j))],
        out_shape=[jax.ShapeDtypeStruct((m, c), BF16),
                   jax.ShapeDtypeStruct((n_seq * tps, 2, c), F32)],
        scratch_shapes=[pltpu.VMEM((HALO_ROWS + L, tn), F32)],
        compiler_params=_params(2),
        name="inproj_a",
    )(x, halo, w_in_a_b, w_in_a_b, w_in_a_b, conv_w_t)
    return main, tile_state.reshape(n_seq, tps, 2, c)[:, tps - 1]


def _attn_out_kernel(x_ref, main_ref, wqm_ref, mk_ref, mv_ref, wo_ref, g_ref, b_ref,
                     yf_ref, yb_ref, mo_scr, *, S, L, alpha):
    xf = x_ref[...]
    qm = _dot(xf.astype(BF16), wqm_ref[...]).astype(BF16)
    n_heads = qm.shape[1] // HEAD_DIM
    scale = 1.0 / math.sqrt(HEAD_DIM)
    for s in range(S):
        rows = slice(s * L, (s + 1) * L)
        for h in range(n_heads):
            cols = slice(h * HEAD_DIM, (h + 1) * HEAD_DIM)
            k = mk_ref[s, :, h, :].astype(BF16)
            v = mv_ref[s, :, h, :].astype(BF16)
            sc = lax.dot_general(qm[rows, cols], k, _NT, preferred_element_type=F32) * scale
            e = jnp.exp(sc - jnp.max(sc, axis=-1, keepdims=True))
            p = e / jnp.sum(e, axis=-1, keepdims=True)
            mo_scr[rows, cols] = _dot(p.astype(BF16), v).astype(BF16)
    d_main = main_ref.shape[1]
    mix = _dot(main_ref[...], wo_ref[0:d_main, :]) + _dot(mo_scr[...], wo_ref[d_main:, :])
    y = _layer_norm(alpha * xf + mix, g_ref[...], b_ref[...])
    yf_ref[...] = y
    yb_ref[...] = y.astype(BF16)


def _attn_out(xf, main, wqm_b, qm_layer, qm_col_blk, mem_k, mem_v, w_out_b, ln_g, ln_b, layer,
              n_seq, seq_len, alpha, *, rows):
    m, d = xf.shape
    d_main = main.shape[1]
    d_mem = w_out_b.shape[1] - d_main
    n_mem, n_mem_heads = mem_k.shape[2], mem_k.shape[3]
    S, L, tps = _seq_tiling(n_seq, seq_len, rows)
    kern = functools.partial(_attn_out_kernel, S=S, L=L, alpha=alpha)
    mem_spec = pl.BlockSpec((None, S, n_mem, n_mem_heads, HEAD_DIM),
                            lambda i: (layer, i // tps, 0, 0, 0))
    ln_spec = pl.BlockSpec((None, 1, d), lambda i: (2 * layer, 0, 0))
    const = dict(pipeline_mode=pl.Buffered(1))
    return pl.pallas_call(
        kern,
        grid=(m // rows,),
        in_specs=[pl.BlockSpec((rows, d), lambda i: (i, 0)),
                  pl.BlockSpec((rows, d_main), lambda i: (i, 0)),
                  pl.BlockSpec((None, d, d_mem), lambda i: (qm_layer, 0, qm_col_blk), **const),
                  mem_spec, mem_spec,
                  pl.BlockSpec((None, d_main + d_mem, d), lambda i: (layer, 0, 0), **const),
                  ln_spec, ln_spec],
        out_specs=[pl.BlockSpec((rows, d), lambda i: (i, 0))] * 2,
        out_shape=[jax.ShapeDtypeStruct((m, d), F32), jax.ShapeDtypeStruct((m, d), BF16)],
        scratch_shapes=[pltpu.VMEM((rows, d_mem), BF16)],
        compiler_params=_params(1),
        name="attn_out",
    )(xf, main, wqm_b, mem_k, mem_v, w_out_b, ln_g, ln_b)


def _mlp_kernel(xb_ref, xf_ref, w1_ref, w2_ref, g_ref, b_ref, yf_ref, yb_ref, acc_ref, *, alpha):
    f = pl.program_id(1)
    h = jnp.square(jnp.maximum(_dot(xb_ref[...], w1_ref[...]), 0.0)).astype(BF16)
    part = _dot(h, w2_ref[...])

    @pl.when(f == 0)
    def _():
        acc_ref[...] = part

    @pl.when(f > 0)
    def _():
        acc_ref[...] += part

    @pl.when(f == pl.num_programs(1) - 1)
    def _():
        y = _layer_norm(alpha * xf_ref[...] + acc_ref[...], g_ref[...], b_ref[...])
        yf_ref[...] = y
        yb_ref[...] = y.astype(BF16)


def _mlp(xb, xf, w_ff1_b, w_ff2_b, ln_g, ln_b, layer, alpha, *, rows=512, tf=512):
    m, d = xf.shape
    d_ff = w_ff1_b.shape[-1]
    ln_spec = pl.BlockSpec((None, 1, d), lambda i, f: (2 * layer + 1, 0, 0))
    return pl.pallas_call(
        functools.partial(_mlp_kernel, alpha=alpha),
        grid=(m // rows, d_ff // tf),
        in_specs=[pl.BlockSpec((rows, d), lambda i, f: (i, 0)),
                  pl.BlockSpec((rows, d), lambda i, f: (i, 0)),
                  pl.BlockSpec((None, d, tf), lambda i, f: (layer, 0, f)),
                  pl.BlockSpec((None, tf, d), lambda i, f: (layer, f, 0)),
                  ln_spec, ln_spec],
        out_specs=[pl.BlockSpec((rows, d), lambda i, f: (i, 0))] * 2,
        out_shape=[jax.ShapeDtypeStruct((m, d), F32), jax.ShapeDtypeStruct((m, d), BF16)],
        scratch_shapes=[pltpu.VMEM((rows, d), F32)],
        compiler_params=_params(2),
        name="mlp",
    )(xb, xf, w_ff1_b, w_ff2_b, ln_g, ln_b)


def _inproj_b_kernel(x_ref, wk_ref, wv_ref, wq_ref, k_ref, v_ref, q_ref, *, S, L):
    x = x_ref[...]
    for w_ref, o_ref in ((wk_ref, k_ref), (wv_ref, v_ref), (wq_ref, q_ref)):
        r = _dot(x, w_ref[...])
        for s in range(S):
            for h in range(o_ref.shape[1]):
                o_ref[s, h] = r[s * L:(s + 1) * L,
                                h * HEAD_DIM:(h + 1) * HEAD_DIM].astype(o_ref.dtype)


def _inproj_b(xb, w_kv_b, w_in_b_b, b_layer, d_main, n_seq, seq_len, *, rows=1024, tn=512):
    m, d = xb.shape
    n_heads = d_main // HEAD_DIM
    nct = d_main // tn
    S, L, tps = _seq_tiling(n_seq, seq_len, rows)
    out_spec = pl.BlockSpec((S, tn // HEAD_DIM, L, HEAD_DIM), lambda i, j: (i // tps, j, i % tps, 0))
    shape = (n_seq, n_heads, seq_len, HEAD_DIM)
    return pl.pallas_call(
        functools.partial(_inproj_b_kernel, S=S, L=L),
        grid=(m // rows, nct),
        in_specs=[pl.BlockSpec((rows, d), lambda i, j: (i, 0)),
                  pl.BlockSpec((d, tn), lambda i, j: (0, j)),
                  pl.BlockSpec((d, tn), lambda i, j: (0, nct + j)),
                  pl.BlockSpec((None, d, tn), lambda i, j: (b_layer, 0, j))],
        out_specs=[out_spec] * 3,
        out_shape=[jax.ShapeDtypeStruct(shape, F32), jax.ShapeDtypeStruct(shape, F32),
                   jax.ShapeDtypeStruct(shape, BF16)],
        compiler_params=_params(2),
        name="inproj_b",
    )(xb, w_kv_b, w_kv_b, w_in_b_b)


def _sb_tile(q, k, v, ones_tri, carried, valid):
    scale = 1.0 / math.sqrt(HEAD_DIM)
    kb = k.shape[0]
    z = lax.dot_general(q, k, _NT, preferred_element_type=F32) * scale
    sp = jnp.maximum(z, 0.0) + jnp.log1p(jnp.exp(-jnp.abs(z)))
    log_a = z - sp
    if valid is not None:
        sp = jnp.where(valid, sp, 0.0)
    hi = sp.astype(BF16)
    lo = (sp - hi.astype(F32)).astype(BF16)
    sums = _dot(hi, ones_tri) + _dot(lo, ones_tri)
    total = sums[:, :HEAD_DIM]
    log_a = log_a - sums[:, HEAD_DIM:HEAD_DIM + kb]
    if carried is not None:
        log_a = log_a - carried[:, :kb]
    a = jnp.exp(log_a)
    if valid is not None:
        a = jnp.where(valid, a, 0.0)
    return _dot(a.astype(BF16), v), total


def _sb_kernel(q_ref, kd_ref, vd_ref, kp_ref, vp_ref, otd_ref, otp_ref, o_ref,
               carried_scr, out_scr, done_ref, *, n_heads, n_past_static):
    j = pl.program_id(2)
    n_past = pl.program_id(1) if n_past_static is None else n_past_static

    def all_underflowed():
        return (jnp.min(carried_scr[...]) > F32_EXP_UNDERFLOW).astype(jnp.int32)

    @pl.when(j == 0)
    def _():
        qb, kb = q_ref.shape[1], kd_ref.shape[1]
        row = lax.broadcasted_iota(jnp.int32, (qb, kb), 0)
        col = lax.broadcasted_iota(jnp.int32, (qb, kb), 1)
        valid = col < row
        for h in range(n_heads):
            cols = slice(h * HEAD_DIM, (h + 1) * HEAD_DIM)
            out, total = _sb_tile(q_ref[h], kd_ref[h].astype(BF16),
                                  vd_ref[h].astype(BF16), otd_ref[...], None, valid)
            out_scr[:, cols] = out
            carried_scr[:, cols] = total
        done_ref[0] = all_underflowed()

    @pl.when((j >= 1) & (j <= n_past) & (done_ref[0] == 0))
    def _():
        for h in range(n_heads):
            cols = slice(h * HEAD_DIM, (h + 1) * HEAD_DIM)
            out, total = _sb_tile(q_ref[h], kp_ref[h].astype(BF16),
                                  vp_ref[h].astype(BF16), otp_ref[...],
                                  carried_scr[:, cols], None)
            out_scr[:, cols] += out
            carried_scr[:, cols] += total
        done_ref[0] = all_underflowed()

    @pl.when(j == pl.num_programs(2) - 1)
    def _():
        o_ref[...] = out_scr[...].astype(o_ref.dtype)


def _ones_tri(kb):
    tri = np.tril(np.ones((kb, kb), np.float32), -1)
    return jnp.asarray(np.concatenate([np.ones((kb, HEAD_DIM), np.float32), tri], axis=1), BF16)


def _stick_breaking(q, k_new, v_new, k_past, v_past):
    n_seq, n_heads, seq_len, _ = q.shape
    c = n_heads * HEAD_DIM
    qb = min(K_BLOCK, seq_len)
    nq = seq_len // qb
    if k_past is None:
        assert qb == K_BLOCK
        k_past, v_past, past_blocks, n_past_static = k_new, v_new, nq, None
        n_steps = nq
    else:
        assert nq == 1 and k_past.shape[2] % K_BLOCK == 0
        past_blocks = k_past.shape[2] // K_BLOCK
        n_past_static = past_blocks
        n_steps = past_blocks + 1

    def past_map(b, i, j):
        n_past = i if n_past_static is None else n_past_static
        return (b, 0, jnp.clip(n_past - j, 0, past_blocks - 1), 0)

    new_spec = pl.BlockSpec((None, n_heads, qb, HEAD_DIM), lambda b, i, j: (b, 0, i, 0))
    past_spec = pl.BlockSpec((None, n_heads, K_BLOCK, HEAD_DIM), past_map)
    otd, otp = _ones_tri(qb), _ones_tri(K_BLOCK)
    return pl.pallas_call(
        functools.partial(_sb_kernel, n_heads=n_heads, n_past_static=n_past_static),
        grid=(n_seq, nq, n_steps),
        in_specs=[new_spec, new_spec, new_spec, past_spec, past_spec,
                  pl.BlockSpec(otd.shape, lambda b, i, j: (0, 0)),
                  pl.BlockSpec(otp.shape, lambda b, i, j: (0, 0))],
        out_specs=pl.BlockSpec((qb, c), lambda b, i, j: (b * nq + i, 0)),
        out_shape=jax.ShapeDtypeStruct((n_seq * seq_len, c), BF16),
        scratch_shapes=[pltpu.VMEM((qb, c), F32), pltpu.VMEM((qb, c), F32),
                        pltpu.SMEM((1,), jnp.int32)],
        compiler_params=_params(3),
        name="stick_breaking",
    )(q, k_new, v_new, k_past, v_past, otd, otp)


def _trunk(x, conv_state, k_past, v_past, mem_k, mem_v, w, *, attn_rows):
    n_seq, seq_len, d = x.shape
    depth, n_a, d_main, alpha = w["depth"], w["n_a"], w["d_main"], w["alpha"]
    xf = x.reshape(n_seq * seq_len, d)
    xb = xf
    conv_states = []
    k_new = v_new = None
    for layer in range(depth):
        if layer < n_a:
            main, st = _inproj_a(xb, conv_state, w["w_in_a"], w["conv_w_t"], layer, n_seq, seq_len)
            conv_states.append(st)
            wqm, qm_layer, qm_col_blk = w["w_in_a"], layer, 3 * d_main // (d - d_main)
        else:
            k, v, q = _inproj_b(xb, w["w_kv"], w["w_in_b"], layer - n_a, d_main, n_seq, seq_len)
            if k_new is None:
                k_new, v_new = k, v
            kp = None if k_past is None else jnp.transpose(k_past, (0, 2, 1, 3))
            vp = None if v_past is None else jnp.transpose(v_past, (0, 2, 1, 3))
            main = _stick_breaking(q, k_new, v_new, kp, vp)
            wqm, qm_layer, qm_col_blk = w["w_in_b"], layer - n_a, d_main // (d - d_main)
        xf, xb = _attn_out(xf, main, wqm, qm_layer, qm_col_blk, mem_k, mem_v, w["w_out"],
                           w["ln_g"], w["ln_b"], layer, n_seq, seq_len, alpha, rows=attn_rows)
        xf, xb = _mlp(xb, xf, w["w_ff1"], w["w_ff2"], w["ln_g"], w["ln_b"], layer, alpha)
    return (xf.reshape(n_seq, seq_len, d), jnp.stack(conv_states),
            jnp.transpose(k_new, (0, 2, 1, 3)), jnp.transpose(v_new, (0, 2, 1, 3)))


def kernel(x_prompt, x_sample, state_conv, cache_k, cache_v, cache_mem_k, cache_mem_v, mem_prompt,
           w_in_a, conv_w, w_in_b, w_kv, w_mem_kv, w_out, w_ff1, w_ff2, ln_g, ln_b):
    depth, d_mix, d = w_out.shape
    n_a, d_main = conv_w.shape[0], conv_w.shape[1]
    d_mem = d_mix - d_main
    n_mem_heads = d_mem // HEAD_DIM
    n_prompt, n_mem = mem_prompt.shape[0], mem_prompt.shape[1]
    w = dict(
        depth=depth, n_a=n_a, d_main=d_main, alpha=(2.0 * depth) ** 0.25,
        w_in_a=w_in_a.astype(BF16), conv_w_t=jnp.swapaxes(conv_w, 1, 2),
        w_in_b=w_in_b.astype(BF16), w_kv=w_kv.astype(BF16), w_out=w_out.astype(BF16),
        w_ff1=w_ff1.astype(BF16), w_ff2=w_ff2.astype(BF16),
        ln_g=ln_g.reshape(depth * 2, 1, d), ln_b=ln_b.reshape(depth * 2, 1, d))

    mem_k_p, mem_v_p = _memkv(mem_prompt.reshape(n_prompt * n_mem, d), w_mem_kv.astype(BF16))
    mem_shape = (depth, n_prompt, n_mem, n_mem_heads, HEAD_DIM)
    mem_k_p = mem_k_p.reshape(mem_shape)
    mem_v_p = mem_v_p.reshape(mem_shape)
    y_p, conv_p, k_p, v_p = _trunk(x_prompt, None, None, None, mem_k_p, mem_v_p, w, attn_rows=512)
    y_s, conv_s, k_s, v_s = _trunk(x_sample, state_conv, cache_k, cache_v, cache_mem_k, cache_mem_v,
                                   w, attn_rows=256)
    return (y_p, y_s, conv_p, conv_s, k_p, v_p, k_s, v_s, mem_k_p, mem_v_p)
```

```python
import functools
import math

import jax
import jax.numpy as jnp
import numpy as np
from jax import lax
from jax.experimental import pallas as pl
from jax.experimental.pallas import tpu as pltpu

F32 = jnp.float32
BF16 = jnp.bfloat16

HEAD_DIM = 128
LN_EPS = 1e-5
K_BLOCK = 128
SB_WINDOW_STEPS = 3
HALO_ROWS = 16
F32_EXP_UNDERFLOW = 104.0
WEIGHT_TILE = 512
VMEM_LIMIT_BYTES = 48 * 1024 * 1024
MLP_VMEM_LIMIT_BYTES = 58 * 1024 * 1024

_NT = (((1,), (1,)), ((), ()))


def _dot(a, b):
    return jnp.dot(a, b, preferred_element_type=F32)


def _params(grid_rank, vmem_limit_bytes=VMEM_LIMIT_BYTES):
    return pltpu.CompilerParams(dimension_semantics=("arbitrary",) * grid_rank,
                                vmem_limit_bytes=vmem_limit_bytes)


def _layer_norm(y, g, b):
    mu = jnp.mean(y, axis=-1, keepdims=True)
    d = y - mu
    var = jnp.mean(d * d, axis=-1, keepdims=True)
    return d * lax.rsqrt(var + LN_EPS) * g + b


def _seq_tiling(n_seq, seq_len, rows):
    if seq_len >= rows:
        assert seq_len % rows == 0
        return 1, rows, seq_len // rows
    assert rows % seq_len == 0 and n_seq % (rows // seq_len) == 0
    return rows // seq_len, seq_len, 1


def _store_heads(ref, val):
    for h in range(ref.shape[1]):
        ref[:, h, :] = val[:, h * HEAD_DIM:(h + 1) * HEAD_DIM].astype(ref.dtype)


def _memkv_kernel(mem_ref, w_ref, k_ref, v_ref):
    r = _dot(mem_ref[...].astype(BF16), w_ref[...])
    d_mem = k_ref.shape[1] * HEAD_DIM
    _store_heads(k_ref, r[:, :d_mem])
    _store_heads(v_ref, r[:, d_mem:])


def _memkv(mem, w_mem_kv_b):
    rows, d = mem.shape
    depth, _, two_dmem = w_mem_kv_b.shape
    n_heads = two_dmem // 2 // HEAD_DIM
    out = jax.ShapeDtypeStruct((depth, rows, n_heads, HEAD_DIM), F32)
    return pl.pallas_call(
        _memkv_kernel,
        grid=(depth,),
        in_specs=[pl.BlockSpec((rows, d), lambda l: (0, 0)),
                  pl.BlockSpec((None, d, two_dmem), lambda l: (l, 0, 0))],
        out_specs=[pl.BlockSpec((None, rows, n_heads, HEAD_DIM), lambda l: (l, 0, 0, 0))] * 2,
        out_shape=[out, out],
        compiler_params=_params(1),
        name="memkv",
    )(mem, w_mem_kv_b)


def _inproj_a_kernel(x_ref, halo_ref, wx_ref, wb_ref, wc_ref, cw_ref, main_ref, st_ref, u_scr,
                     *, S, L, tiles_per_seq, has_state):
    x = x_ref[...].astype(BF16)
    wx = wx_ref[...]
    wc = wc_ref[...]
    xin = _dot(x, wx)
    gate_b = _dot(x, wb_ref[...])
    gate_c = _dot(x, wc)
    u = gate_c * xin
    cw = cw_ref[...]
    w0, w1, w2 = cw[0:1], cw[1:2], cw[2:3]
    if not has_state:
        xh = halo_ref[...].astype(BF16)
        uh = _dot(xh, wc) * _dot(xh, wx)
        first = (pl.program_id(0) % tiles_per_seq) == 0
        u_scr[0:HALO_ROWS, :] = jnp.where(first, 0.0, uh)
    for s in range(S):
        rows = slice(s * L, (s + 1) * L)
        if has_state:
            u_scr[HALO_ROWS - 2:HALO_ROWS, :] = halo_ref[s]
        u_scr[HALO_ROWS:HALO_ROWS + L, :] = u[rows]
        y = u_scr[HALO_ROWS - 2:HALO_ROWS - 2 + L, :] * w0
        y = y + u_scr[HALO_ROWS - 1:HALO_ROWS - 1 + L, :] * w1
        y = y + u[rows] * w2
        main_ref[rows, :] = (gate_b[rows] * y).astype(main_ref.dtype)
        st_ref[s] = u_scr[HALO_ROWS + L - 2:HALO_ROWS + L, :]


def _inproj_a(x, state, w_in_a_t, conv_w_t, layer, n_seq, seq_len, *, rows=1024):
    m, d = x.shape
    c = conv_w_t.shape[-1]
    tn = WEIGHT_TILE
    S, L, tps = _seq_tiling(n_seq, seq_len, rows)
    nct = c // tn
    has_state = state is not None
    if has_state:
        assert tps == 1
        halo = state
        halo_spec = pl.BlockSpec((None, S, 2, tn), lambda i, j: (layer, i, 0, j))
    else:
        halo = x
        halo_spec = pl.BlockSpec(
            (HALO_ROWS, d), lambda i, j: (jnp.maximum(i * (rows // HALO_ROWS) - 1, 0), 0))
    kern = functools.partial(_inproj_a_kernel, S=S, L=L, tiles_per_seq=tps, has_state=has_state)
    w_spec = lambda off: pl.BlockSpec((None, d, tn), lambda i, j: (layer, 0, off + j))
    main, tile_state = pl.pallas_call(
        kern,
        grid=(m // rows, nct),
        in_specs=[pl.BlockSpec((rows, d), lambda i, j: (i, 0)),
                  halo_spec,
                  w_spec(0), w_spec(nct), w_spec(2 * nct),
                  pl.BlockSpec((None, 3, tn), lambda i, j: (layer, 0, j))],
        out_specs=[pl.BlockSpec((rows, tn), lambda i, j: (i, j)),
                   pl.BlockSpec((S, 2, tn), lambda i, j: (i, 0, j))],
        out_shape=[jax.ShapeDtypeStruct((m, c), BF16),
                   jax.ShapeDtypeStruct((n_seq * tps, 2, c), F32)],
        scratch_shapes=[pltpu.VMEM((HALO_ROWS + L, tn), F32)],
        compiler_params=_params(2),
        name="inproj_a",
    )(x, halo, w_in_a_t, w_in_a_t, w_in_a_t, conv_w_t)
    return main, tile_state.reshape(n_seq, tps, 2, c)[:, tps - 1]


def _attn_out_kernel(x_ref, main_ref, wqm_ref, mk_ref, mv_ref, wo_ref, g_ref, b_ref,
                     yf_ref, yb_ref, mo_scr, *, S, L, alpha):
    xf = x_ref[...]
    qm = _dot(xf.astype(BF16), wqm_ref[...]).astype(BF16)
    n_heads = qm.shape[1] // HEAD_DIM
    scale = 1.0 / math.sqrt(HEAD_DIM)
    for s in range(S):
        rows = slice(s * L, (s + 1) * L)
        for h in range(n_heads):
            cols = slice(h * HEAD_DIM, (h + 1) * HEAD_DIM)
            k = mk_ref[s, :, h, :].astype(BF16)
            v = mv_ref[s, :, h, :].astype(BF16)
            sc = lax.dot_general(qm[rows, cols], k, _NT, preferred_element_type=F32) * scale
            e = jnp.exp(sc - jnp.max(sc, axis=-1, keepdims=True))
            p = e / jnp.sum(e, axis=-1, keepdims=True)
            mo_scr[rows, cols] = _dot(p.astype(BF16), v).astype(BF16)
    d_main = main_ref.shape[1]
    mix = _dot(main_ref[...], wo_ref[0:d_main, :]) + _dot(mo_scr[...], wo_ref[d_main:, :])
    y = _layer_norm(alpha * xf + mix, g_ref[...], b_ref[...])
    yf_ref[...] = y
    yb_ref[...] = y.astype(BF16)


def _attn_out(xf, main, wqm_t, qm_layer, qm_col_blk, mem_k, mem_v, w_out_b, ln_g, ln_b, layer,
              n_seq, seq_len, alpha, *, rows):
    m, d = xf.shape
    d_main = main.shape[1]
    d_mem = w_out_b.shape[1] - d_main
    assert d_mem == WEIGHT_TILE
    n_mem, n_mem_heads = mem_k.shape[2], mem_k.shape[3]
    S, L, tps = _seq_tiling(n_seq, seq_len, rows)
    kern = functools.partial(_attn_out_kernel, S=S, L=L, alpha=alpha)
    mem_spec = pl.BlockSpec((None, S, n_mem, n_mem_heads, HEAD_DIM),
                            lambda i: (layer, i // tps, 0, 0, 0))
    ln_spec = pl.BlockSpec((None, 1, d), lambda i: (2 * layer, 0, 0))
    const = dict(pipeline_mode=pl.Buffered(1))
    return pl.pallas_call(
        kern,
        grid=(m // rows,),
        in_specs=[pl.BlockSpec((rows, d), lambda i: (i, 0)),
                  pl.BlockSpec((rows, d_main), lambda i: (i, 0)),
                  pl.BlockSpec((None, d, d_mem), lambda i: (qm_layer, 0, qm_col_blk), **const),
                  mem_spec, mem_spec,
                  pl.BlockSpec((None, d_main + d_mem, d), lambda i: (layer, 0, 0), **const),
                  ln_spec, ln_spec],
        out_specs=[pl.BlockSpec((rows, d), lambda i: (i, 0))] * 2,
        out_shape=[jax.ShapeDtypeStruct((m, d), F32), jax.ShapeDtypeStruct((m, d), BF16)],
        scratch_shapes=[pltpu.VMEM((rows, d_mem), BF16)],
        compiler_params=_params(1),
        name="attn_out",
    )(xf, main, wqm_t, mem_k, mem_v, w_out_b, ln_g, ln_b)


def _mlp_kernel(xb_ref, xf_ref, w1_ref, w2_ref, g_ref, b_ref, yf_ref, *yb_ref, alpha):
    f = pl.program_id(1)

    @pl.when(f == 0)
    def _():
        yf_ref[...] = jnp.zeros_like(yf_ref)

    h = jnp.square(jnp.maximum(_dot(xb_ref[...], w1_ref[...]), 0.0)).astype(BF16)
    yf_ref[...] += _dot(h, w2_ref[...])

    @pl.when(f == pl.num_programs(1) - 1)
    def _():
        y = _layer_norm(alpha * xf_ref[...] + yf_ref[...], g_ref[...], b_ref[...])
        yf_ref[...] = y
        for ref in yb_ref:
            ref[...] = y.astype(BF16)


def _mlp(xb, xf, w_ff1_t, w_ff2_b, ln_g, ln_b, layer, alpha, *, want_bf16, rows=1024):
    m, d = xf.shape
    tf = WEIGHT_TILE
    n_f = w_ff1_t.shape[2] // tf
    ln_spec = pl.BlockSpec((None, 1, d), lambda i, f: (2 * layer + 1, 0, 0))
    row_spec = pl.BlockSpec((rows, d), lambda i, f: (i, 0))
    out_shape = [jax.ShapeDtypeStruct((m, d), F32)] + [jax.ShapeDtypeStruct((m, d), BF16)] * want_bf16
    out = pl.pallas_call(
        functools.partial(_mlp_kernel, alpha=alpha),
        grid=(m // rows, n_f),
        in_specs=[row_spec,
                  pl.BlockSpec((rows, d), lambda i, f: (i, 0), pipeline_mode=pl.Buffered(1)),
                  pl.BlockSpec((None, d, tf), lambda i, f: (layer, 0, f)),
                  pl.BlockSpec((None, tf, d), lambda i, f: (layer, f, 0)),
                  ln_spec, ln_spec],
        out_specs=[row_spec] * len(out_shape),
        out_shape=out_shape,
        compiler_params=_params(2, MLP_VMEM_LIMIT_BYTES),
        name="mlp",
    )(xb, xf, w_ff1_t, w_ff2_b, ln_g, ln_b)
    return out if want_bf16 else (out[0], None)


def _inproj_b_kernel(x_ref, wk_ref, wv_ref, wq_ref, k_ref, v_ref, q_ref, *, S, L):
    x = x_ref[...]
    for w_ref, o_ref in ((wk_ref, k_ref), (wv_ref, v_ref), (wq_ref, q_ref)):
        r = _dot(x, w_ref[...])
        for s in range(S):
            for h in range(o_ref.shape[1]):
                o_ref[s, h] = r[s * L:(s + 1) * L,
                                h * HEAD_DIM:(h + 1) * HEAD_DIM].astype(o_ref.dtype)


def _inproj_b(xb, w_kv_t, w_in_b_t, b_layer, d_main, n_seq, seq_len, *, rows=1024):
    m, d = xb.shape
    n_heads = d_main // HEAD_DIM
    tn = WEIGHT_TILE
    nct = d_main // tn
    S, L, tps = _seq_tiling(n_seq, seq_len, rows)
    out_spec = pl.BlockSpec((S, tn // HEAD_DIM, L, HEAD_DIM), lambda i, j: (i // tps, j, i % tps, 0))
    shape = (n_seq, n_heads, seq_len, HEAD_DIM)
    return pl.pallas_call(
        functools.partial(_inproj_b_kernel, S=S, L=L),
        grid=(m // rows, nct),
        in_specs=[pl.BlockSpec((rows, d), lambda i, j: (i, 0)),
                  pl.BlockSpec((d, tn), lambda i, j: (0, j)),
                  pl.BlockSpec((d, tn), lambda i, j: (0, nct + j)),
                  pl.BlockSpec((None, d, tn), lambda i, j: (b_layer, 0, j))],
        out_specs=[out_spec] * 3,
        out_shape=[jax.ShapeDtypeStruct(shape, F32), jax.ShapeDtypeStruct(shape, F32),
                   jax.ShapeDtypeStruct(shape, BF16)],
        compiler_params=_params(2),
        name="inproj_b",
    )(xb, w_kv_t, w_kv_t, w_in_b_t)


def _softplus(z):
    return jnp.maximum(z, 0.0) + jnp.log1p(jnp.exp(-jnp.abs(z)))


def _sb_block(q_ref, k_ref, v_ref, ones_tri, carried_scr, out_scr, *, newest):
    n_heads, qb, _ = q_ref.shape
    kb = k_ref.shape[1]
    scale = 1.0 / math.sqrt(HEAD_DIM)
    z = jnp.concatenate(
        [lax.dot_general(q_ref[h], k_ref[h].astype(BF16), _NT, preferred_element_type=F32)
         for h in range(n_heads)], axis=0) * scale
    sp = _softplus(z)
    log_a = z - sp
    if newest:
        row = lax.broadcasted_iota(jnp.int32, z.shape, 0) & (qb - 1)
        col = lax.broadcasted_iota(jnp.int32, z.shape, 1)
        valid = col < row
        sp = jnp.where(valid, sp, 0.0)
    hi = sp.astype(BF16)
    lo = (sp - hi.astype(F32)).astype(BF16)
    sums = _dot(hi, ones_tri) + _dot(lo, ones_tri)
    log_a = log_a - sums[:, HEAD_DIM:HEAD_DIM + kb]
    if not newest:
        log_a = log_a - carried_scr[:, :kb]
    a = jnp.exp(log_a)
    if newest:
        a = jnp.where(valid, a, 0.0)
    a = a.astype(BF16)
    for h in range(n_heads):
        contrib = _dot(a[h * qb:(h + 1) * qb], v_ref[h].astype(BF16))
        if newest:
            out_scr[h] = contrib
        else:
            out_scr[h] += contrib
    if newest:
        carried_scr[...] = sums[:, :HEAD_DIM]
    else:
        carried_scr[...] += sums[:, :HEAD_DIM]


def _sb_kernel(q_ref, kd_ref, vd_ref, kp_ref, vp_ref, otd_ref, otp_ref, k_hbm, v_hbm, o_ref,
               carried_scr, out_scr, k_buf, v_buf, sem, done_ref, *, n_steps, n_past_static):
    b = pl.program_id(0)
    j = pl.program_id(2)
    n_past = pl.program_id(1) if n_past_static is None else n_past_static

    def all_underflowed():
        return (jnp.min(carried_scr[...]) > F32_EXP_UNDERFLOW).astype(jnp.int32)

    @pl.when(j == 0)
    def _():
        _sb_block(q_ref, kd_ref, vd_ref, otd_ref[...], carried_scr, out_scr, newest=True)
        done_ref[0] = all_underflowed()

    @pl.when((j >= 1) & (j <= n_past) & (done_ref[0] == 0))
    def _():
        _sb_block(q_ref, kp_ref, vp_ref, otp_ref[...], carried_scr, out_scr, newest=False)
        done_ref[0] = all_underflowed()

    @pl.when(j == n_steps - 1)
    def _():
        n_left = n_past - (n_steps - 1)

        def fetch(hbm, buf, slot, blk):
            start = pl.multiple_of(blk * K_BLOCK, K_BLOCK)
            return pltpu.make_async_copy(hbm.at[b, :, pl.ds(start, K_BLOCK), :], buf, sem.at[slot])

        def body(carry):
            i, _ = carry
            blk = n_left - 1 - i
            fetch(k_hbm, k_buf, 0, blk).start()
            fetch(v_hbm, v_buf, 1, blk).start()
            fetch(k_hbm, k_buf, 0, blk).wait()
            fetch(v_hbm, v_buf, 1, blk).wait()
            _sb_block(q_ref, k_buf, v_buf, otp_ref[...], carried_scr, out_scr, newest=False)
            return i + 1, all_underflowed()

        lax.while_loop(lambda c: (c[0] < n_left) & (c[1] == 0), body, (jnp.int32(0), done_ref[0]))
        for h in range(out_scr.shape[0]):
            o_ref[:, h * HEAD_DIM:(h + 1) * HEAD_DIM] = out_scr[h].astype(o_ref.dtype)


def _ones_tri(kb):
    tri = np.tril(np.ones((kb, kb), np.float32), -1)
    return jnp.asarray(np.concatenate([np.ones((kb, HEAD_DIM), np.float32), tri], axis=1), BF16)


def _stick_breaking(q, k_new, v_new, k_past, v_past):
    n_seq, n_heads, seq_len, _ = q.shape
    c = n_heads * HEAD_DIM
    qb = min(K_BLOCK, seq_len)
    assert qb & (qb - 1) == 0
    nq = seq_len // qb
    if k_past is None:
        assert qb == K_BLOCK
        k_past, v_past, past_blocks, n_past_static = k_new, v_new, nq, None
        n_steps = min(SB_WINDOW_STEPS, nq)
    else:
        assert nq == 1 and k_past.shape[2] % K_BLOCK == 0
        past_blocks = k_past.shape[2] // K_BLOCK
        n_past_static = past_blocks
        n_steps = min(SB_WINDOW_STEPS, past_blocks + 1)

    def past_map(b, i, j):
        n_past = i if n_past_static is None else n_past_static
        return (b, 0, jnp.clip(n_past - j, 0, past_blocks - 1), 0)

    new_spec = pl.BlockSpec((None, n_heads, qb, HEAD_DIM), lambda b, i, j: (b, 0, i, 0))
    past_spec = pl.BlockSpec((None, n_heads, K_BLOCK, HEAD_DIM), past_map)
    hbm_spec = pl.BlockSpec(memory_space=pl.ANY)
    otd, otp = _ones_tri(qb), _ones_tri(K_BLOCK)
    return pl.pallas_call(
        functools.partial(_sb_kernel, n_steps=n_steps, n_past_static=n_past_static),
        grid=(n_seq, nq, n_steps),
        in_specs=[new_spec, new_spec, new_spec, past_spec, past_spec,
                  pl.BlockSpec(otd.shape, lambda b, i, j: (0, 0)),
                  pl.BlockSpec(otp.shape, lambda b, i, j: (0, 0)),
                  hbm_spec, hbm_spec],
        out_specs=pl.BlockSpec((qb, c), lambda b, i, j: (b * nq + i, 0)),
        out_shape=jax.ShapeDtypeStruct((n_seq * seq_len, c), BF16),
        scratch_shapes=[pltpu.VMEM((n_heads * qb, HEAD_DIM), F32),
                        pltpu.VMEM((n_heads, qb, HEAD_DIM), F32),
                        pltpu.VMEM((n_heads, K_BLOCK, HEAD_DIM), k_past.dtype),
                        pltpu.VMEM((n_heads, K_BLOCK, HEAD_DIM), v_past.dtype),
                        pltpu.SemaphoreType.DMA((2,)),
                        pltpu.SMEM((1,), jnp.int32)],
        compiler_params=_params(3),
        name="stick_breaking",
    )(q, k_new, v_new, k_past, v_past, otd, otp, k_past, v_past)


def _trunk(x, conv_state, k_past, v_past, mem_k, mem_v, w, *, attn_rows):
    n_seq, seq_len, d = x.shape
    depth, n_a, d_main, alpha = w["depth"], w["n_a"], w["d_main"], w["alpha"]
    xf = x.reshape(n_seq * seq_len, d)
    xb = xf
    conv_states = []
    k_new = v_new = None
    for layer in range(depth):
        if layer < n_a:
            main, st = _inproj_a(xb, conv_state, w["w_in_a"], w["conv_w_t"], layer, n_seq, seq_len)
            conv_states.append(st)
            wqm, qm_layer, qm_col_blk = w["w_in_a"], layer, 3 * d_main // (d - d_main)
        else:
            k, v, q = _inproj_b(xb, w["w_kv"], w["w_in_b"], layer - n_a, d_main, n_seq, seq_len)
            if k_new is None:
                k_new, v_new = k, v
            kp = None if k_past is None else jnp.transpose(k_past, (0, 2, 1, 3))
            vp = None if v_past is None else jnp.transpose(v_past, (0, 2, 1, 3))
            main = _stick_breaking(q, k_new, v_new, kp, vp)
            wqm, qm_layer, qm_col_blk = w["w_in_b"], layer - n_a, d_main // (d - d_main)
        xf, xb = _attn_out(xf, main, wqm, qm_layer, qm_col_blk, mem_k, mem_v, w["w_out"],
                           w["ln_g"], w["ln_b"], layer, n_seq, seq_len, alpha, rows=attn_rows)
        xf, xb = _mlp(xb, xf, w["w_ff1"], w["w_ff2"], w["ln_g"], w["ln_b"], layer, alpha,
                      want_bf16=layer + 1 < depth)
    return (xf.reshape(n_seq, seq_len, d), jnp.stack(conv_states),
            jnp.transpose(k_new, (0, 2, 1, 3)), jnp.transpose(v_new, (0, 2, 1, 3)))


def kernel(x_prompt, x_sample, state_conv, cache_k, cache_v, cache_mem_k, cache_mem_v, mem_prompt,
           w_in_a, conv_w, w_in_b, w_kv, w_mem_kv, w_out, w_ff1, w_ff2, ln_g, ln_b):
    depth, d_mix, d = w_out.shape
    n_a, d_main = conv_w.shape[0], conv_w.shape[1]
    d_mem = d_mix - d_main
    n_mem_heads = d_mem // HEAD_DIM
    n_prompt, n_mem = mem_prompt.shape[0], mem_prompt.shape[1]
    w = dict(
        depth=depth, n_a=n_a, d_main=d_main, alpha=(2.0 * depth) ** 0.25,
        w_in_a=w_in_a.astype(BF16), conv_w_t=jnp.swapaxes(conv_w, 1, 2),
        w_in_b=w_in_b.astype(BF16), w_kv=w_kv.astype(BF16), w_out=w_out.astype(BF16),
        w_ff1=w_ff1.astype(BF16), w_ff2=w_ff2.astype(BF16),
        ln_g=ln_g.reshape(depth * 2, 1, d), ln_b=ln_b.reshape(depth * 2, 1, d))

    mem_k_p, mem_v_p = _memkv(mem_prompt.reshape(n_prompt * n_mem, d), w_mem_kv.astype(BF16))
    mem_shape = (depth, n_prompt, n_mem, n_mem_heads, HEAD_DIM)
    mem_k_p = mem_k_p.reshape(mem_shape)
    mem_v_p = mem_v_p.reshape(mem_shape)
    y_p, conv_p, k_p, v_p = _trunk(x_prompt, None, None, None, mem_k_p, mem_v_p, w, attn_rows=512)
    y_s, conv_s, k_s, v_s = _trunk(x_sample, state_conv, cache_k, cache_v, cache_mem_k, cache_mem_v,
                                   w, attn_rows=256)
    return (y_p, y_s, conv_p, conv_s, k_p, v_p, k_s, v_s, mem_k_p, mem_v_p)
```

```python
import functools
import math

import jax
import jax.numpy as jnp
import numpy as np
from jax import lax
from jax.experimental import pallas as pl
from jax.experimental.pallas import tpu as pltpu

F32 = jnp.float32
BF16 = jnp.bfloat16

HEAD_DIM = 128
LN_EPS = 1e-5
K_BLOCK = 128
SB_WINDOW_STEPS = 3
HALO_ROWS = 16
F32_EXP_UNDERFLOW = 104.0
WEIGHT_TILE = 512
VMEM_LIMIT_BYTES = 48 * 1024 * 1024
MLP_VMEM_LIMIT_BYTES = 58 * 1024 * 1024

_NT = (((1,), (1,)), ((), ()))


def _dot(a, b):
    return jnp.dot(a, b, preferred_element_type=F32)


def _params(grid_rank, vmem_limit_bytes=VMEM_LIMIT_BYTES):
    return pltpu.CompilerParams(dimension_semantics=("arbitrary",) * grid_rank,
                                vmem_limit_bytes=vmem_limit_bytes)


def _layer_norm(y, g, b):
    mu = jnp.mean(y, axis=-1, keepdims=True)
    d = y - mu
    var = jnp.mean(d * d, axis=-1, keepdims=True)
    return d * lax.rsqrt(var + LN_EPS) * g + b


def _seq_tiling(n_seq, seq_len, rows):
    if seq_len >= rows:
        assert seq_len % rows == 0
        return 1, rows, seq_len // rows
    assert rows % seq_len == 0 and n_seq % (rows // seq_len) == 0
    return rows // seq_len, seq_len, 1


def _memkv_kernel(mem_ref, w_ref, k_ref, v_ref, *, n_heads):
    rows = mem_ref.shape[0]
    r = _dot(mem_ref[...].astype(BF16), w_ref[...])
    for i, ref in enumerate((k_ref, v_ref)):
        for h in range(n_heads):
            col = (i * n_heads + h) * HEAD_DIM
            ref[pl.ds(h, rows, stride=n_heads), :] = r[:, col:col + HEAD_DIM]


def _memkv(mem, w_mem_kv_b):
    rows, d = mem.shape
    depth, _, two_dmem = w_mem_kv_b.shape
    n_heads = two_dmem // 2 // HEAD_DIM
    out = jax.ShapeDtypeStruct((depth, rows * n_heads, HEAD_DIM), F32)
    return pl.pallas_call(
        functools.partial(_memkv_kernel, n_heads=n_heads),
        grid=(depth,),
        in_specs=[pl.BlockSpec((rows, d), lambda l: (0, 0)),
                  pl.BlockSpec((None, d, two_dmem), lambda l: (l, 0, 0))],
        out_specs=[pl.BlockSpec((None, rows * n_heads, HEAD_DIM), lambda l: (l, 0, 0))] * 2,
        out_shape=[out, out],
        compiler_params=_params(1),
        name="memkv",
    )(mem, w_mem_kv_b)


def _inproj_a_kernel(x_ref, halo_ref, wx_ref, wb_ref, wc_ref, cw_ref, wqm_ref,
                     main_ref, st_ref, qm_ref, u_scr, *, S, L, tiles_per_seq, has_state, nct):
    j = pl.program_id(1)

    @pl.when(j < nct)
    def _():
        x = x_ref[...].astype(BF16)
        wx = wx_ref[...]
        wc = wc_ref[...]
        xin = _dot(x, wx)
        gate_b = _dot(x, wb_ref[...])
        gate_c = _dot(x, wc)
        u = gate_c * xin
        cw = cw_ref[...]
        w0, w1, w2 = cw[0:1], cw[1:2], cw[2:3]
        if not has_state:
            xh = halo_ref[...].astype(BF16)
            uh = _dot(xh, wc) * _dot(xh, wx)
            first = (pl.program_id(0) % tiles_per_seq) == 0
            u_scr[0:HALO_ROWS, :] = jnp.where(first, 0.0, uh)
        for s in range(S):
            rows = slice(s * L, (s + 1) * L)
            if has_state:
                u_scr[HALO_ROWS - 2:HALO_ROWS, :] = halo_ref[s]
            u_scr[HALO_ROWS:HALO_ROWS + L, :] = u[rows]
            y = u_scr[HALO_ROWS - 2:HALO_ROWS - 2 + L, :] * w0
            y = y + u_scr[HALO_ROWS - 1:HALO_ROWS - 1 + L, :] * w1
            y = y + u[rows] * w2
            main_ref[rows, :] = (gate_b[rows] * y).astype(main_ref.dtype)
            st_ref[s] = u_scr[HALO_ROWS + L - 2:HALO_ROWS + L, :]

    @pl.when(j == nct)
    def _():
        qm_ref[...] = _dot(x_ref[...].astype(BF16), wqm_ref[...]).astype(BF16)


def _inproj_a(x, state, w_in_a_t, conv_w_t, layer, n_seq, seq_len, *, rows=1024):
    m, d = x.shape
    c = conv_w_t.shape[-1]
    tn = WEIGHT_TILE
    S, L, tps = _seq_tiling(n_seq, seq_len, rows)
    nct = c // tn
    has_state = state is not None
    cj = lambda j: jnp.minimum(j, nct - 1)
    if has_state:
        assert tps == 1
        halo = state
        halo_spec = pl.BlockSpec((None, S, 2, tn), lambda i, j: (layer, i, 0, cj(j)))
    else:
        halo = x
        halo_spec = pl.BlockSpec(
            (HALO_ROWS, d), lambda i, j: (jnp.maximum(i * (rows // HALO_ROWS) - 1, 0), 0))
    kern = functools.partial(_inproj_a_kernel, S=S, L=L, tiles_per_seq=tps, has_state=has_state,
                             nct=nct)
    w_spec = lambda off: pl.BlockSpec((None, d, tn), lambda i, j: (layer, 0, off + cj(j)))
    main, tile_state, q_mem = pl.pallas_call(
        kern,
        grid=(m // rows, nct + 1),
        in_specs=[pl.BlockSpec((rows, d), lambda i, j: (i, 0)),
                  halo_spec,
                  w_spec(0), w_spec(nct), w_spec(2 * nct),
                  pl.BlockSpec((None, 3, tn), lambda i, j: (layer, 0, cj(j))),
                  pl.BlockSpec((None, d, tn), lambda i, j: (layer, 0, 3 * nct))],
        out_specs=[pl.BlockSpec((rows, tn), lambda i, j: (i, cj(j))),
                   pl.BlockSpec((S, 2, tn), lambda i, j: (i, 0, cj(j))),
                   pl.BlockSpec((rows, tn), lambda i, j: (i, 0))],
        out_shape=[jax.ShapeDtypeStruct((m, c), BF16),
                   jax.ShapeDtypeStruct((n_seq * tps, 2, c), F32),
                   jax.ShapeDtypeStruct((m, tn), BF16)],
        scratch_shapes=[pltpu.VMEM((HALO_ROWS + L, tn), F32)],
        compiler_params=_params(2),
        name="inproj_a",
    )(x, halo, w_in_a_t, w_in_a_t, w_in_a_t, conv_w_t, w_in_a_t)
    return main, tile_state.reshape(n_seq, tps, 2, c)[:, tps - 1], q_mem


def _mem_attn_kernel(q_ref, mk_ref, mv_ref, o_ref, *, S, L):
    n_heads = q_ref.shape[1] // HEAD_DIM
    n_mem = mk_ref.shape[1] // n_heads
    scale = 1.0 / math.sqrt(HEAD_DIM)
    tiles = [(slice(s * L, (s + 1) * L), slice(h * HEAD_DIM, (h + 1) * HEAD_DIM),
              s, pl.ds(h, n_mem, stride=n_heads))
             for s in range(S) for h in range(n_heads)]
    sc = jnp.concatenate(
        [lax.dot_general(q_ref[rows, cols], mk_ref[s, head_rows, :].astype(BF16), _NT,
                         preferred_element_type=F32) for rows, cols, s, head_rows in tiles],
        axis=0) * scale
    e = jnp.exp(sc - jnp.max(sc, axis=-1, keepdims=True))
    p = (e / jnp.sum(e, axis=-1, keepdims=True)).astype(BF16)
    for i, (rows, cols, s, head_rows) in enumerate(tiles):
        v = mv_ref[s, head_rows, :].astype(BF16)
        o_ref[rows, cols] = _dot(p[i * L:(i + 1) * L], v).astype(BF16)


def _mem_attn(q_mem, mem_k, mem_v, layer, n_seq, seq_len, *, rows):
    m, d_mem = q_mem.shape
    S, L, tps = _seq_tiling(n_seq, seq_len, rows)
    mem_spec = pl.BlockSpec((None, S) + mem_k.shape[2:], lambda i: (layer, i // tps, 0, 0))
    row_spec = pl.BlockSpec((rows, d_mem), lambda i: (i, 0))
    return pl.pallas_call(
        functools.partial(_mem_attn_kernel, S=S, L=L),
        grid=(m // rows,),
        in_specs=[row_spec, mem_spec, mem_spec],
        out_specs=row_spec,
        out_shape=jax.ShapeDtypeStruct((m, d_mem), BF16),
        compiler_params=_params(1),
        name="mem_attn",
    )(q_mem, mem_k, mem_v)


def _out_proj_kernel(x_ref, main_ref, mo_ref, wo_ref, g_ref, b_ref, yf_ref, yb_ref, *, alpha):
    d_main = main_ref.shape[1]
    mix = _dot(main_ref[...], wo_ref[0:d_main, :]) + _dot(mo_ref[...], wo_ref[d_main:, :])
    y = _layer_norm(alpha * x_ref[...] + mix, g_ref[...], b_ref[...])
    yf_ref[...] = y
    yb_ref[...] = y.astype(BF16)


def _out_proj(xf, main, mo, w_out_b, ln_g, ln_b, layer, alpha, *, rows=512):
    m, d = xf.shape
    d_main, d_mem = main.shape[1], mo.shape[1]
    ln_spec = pl.BlockSpec((None, 1, d), lambda i: (2 * layer, 0, 0))
    return pl.pallas_call(
        functools.partial(_out_proj_kernel, alpha=alpha),
        grid=(m // rows,),
        in_specs=[pl.BlockSpec((rows, d), lambda i: (i, 0)),
                  pl.BlockSpec((rows, d_main), lambda i: (i, 0)),
                  pl.BlockSpec((rows, d_mem), lambda i: (i, 0)),
                  pl.BlockSpec((None, d_main + d_mem, d), lambda i: (layer, 0, 0),
                               pipeline_mode=pl.Buffered(1)),
                  ln_spec, ln_spec],
        out_specs=[pl.BlockSpec((rows, d), lambda i: (i, 0))] * 2,
        out_shape=[jax.ShapeDtypeStruct((m, d), F32), jax.ShapeDtypeStruct((m, d), BF16)],
        compiler_params=_params(1),
        name="out_proj",
    )(xf, main, mo, w_out_b, ln_g, ln_b)


def _mlp_kernel(xb_ref, xf_ref, w1_ref, w2_ref, g_ref, b_ref, yf_ref, *yb_ref, alpha):
    f = pl.program_id(1)

    @pl.when(f == 0)
    def _():
        yf_ref[...] = jnp.zeros_like(yf_ref)

    h = jnp.square(jnp.maximum(_dot(xb_ref[...], w1_ref[...]), 0.0)).astype(BF16)
    yf_ref[...] += _dot(h, w2_ref[...])

    @pl.when(f == pl.num_programs(1) - 1)
    def _():
        y = _layer_norm(alpha * xf_ref[...] + yf_ref[...], g_ref[...], b_ref[...])
        yf_ref[...] = y
        for ref in yb_ref:
            ref[...] = y.astype(BF16)


def _mlp(xb, xf, w_ff1_t, w_ff2_b, ln_g, ln_b, layer, alpha, *, want_bf16, rows=1024):
    m, d = xf.shape
    tf = WEIGHT_TILE
    n_f = w_ff1_t.shape[2] // tf
    ln_spec = pl.BlockSpec((None, 1, d), lambda i, f: (2 * layer + 1, 0, 0))
    row_spec = pl.BlockSpec((rows, d), lambda i, f: (i, 0))
    out_shape = [jax.ShapeDtypeStruct((m, d), F32)] + [jax.ShapeDtypeStruct((m, d), BF16)] * want_bf16
    out = pl.pallas_call(
        functools.partial(_mlp_kernel, alpha=alpha),
        grid=(m // rows, n_f),
        in_specs=[row_spec,
                  pl.BlockSpec((rows, d), lambda i, f: (i, 0), pipeline_mode=pl.Buffered(1)),
                  pl.BlockSpec((None, d, tf), lambda i, f: (layer, 0, f)),
                  pl.BlockSpec((None, tf, d), lambda i, f: (layer, f, 0)),
                  ln_spec, ln_spec],
        out_specs=[row_spec] * len(out_shape),
        out_shape=out_shape,
        compiler_params=_params(2, MLP_VMEM_LIMIT_BYTES),
        name="mlp",
    )(xb, xf, w_ff1_t, w_ff2_b, ln_g, ln_b)
    return out if want_bf16 else (out[0], None)


def _inproj_b_kernel(x_ref, wk_ref, wv_ref, wq_ref, k_ref, v_ref, q_ref, qm_ref, *, S, L, nct):
    j = pl.program_id(1)

    @pl.when(j < nct)
    def _():
        x = x_ref[...]
        for w_ref, o_ref in ((wk_ref, k_ref), (wv_ref, v_ref), (wq_ref, q_ref)):
            r = _dot(x, w_ref[...])
            for s in range(S):
                for h in range(o_ref.shape[1]):
                    o_ref[s, h] = r[s * L:(s + 1) * L,
                                    h * HEAD_DIM:(h + 1) * HEAD_DIM].astype(o_ref.dtype)

    @pl.when(j == nct)
    def _():
        qm_ref[...] = _dot(x_ref[...], wq_ref[...]).astype(BF16)


def _inproj_b(xb, w_kv_t, w_in_b_t, b_layer, d_main, n_seq, seq_len, *, rows=1024):
    m, d = xb.shape
    n_heads = d_main // HEAD_DIM
    tn = WEIGHT_TILE
    nct = d_main // tn
    assert w_in_b_t.shape[-1] == (nct + 1) * tn
    S, L, tps = _seq_tiling(n_seq, seq_len, rows)
    cj = lambda j: jnp.minimum(j, nct - 1)
    out_spec = pl.BlockSpec((S, tn // HEAD_DIM, L, HEAD_DIM),
                            lambda i, j: (i // tps, cj(j), i % tps, 0))
    shape = (n_seq, n_heads, seq_len, HEAD_DIM)
    return pl.pallas_call(
        functools.partial(_inproj_b_kernel, S=S, L=L, nct=nct),
        grid=(m // rows, nct + 1),
        in_specs=[pl.BlockSpec((rows, d), lambda i, j: (i, 0)),
                  pl.BlockSpec((d, tn), lambda i, j: (0, cj(j))),
                  pl.BlockSpec((d, tn), lambda i, j: (0, nct + cj(j))),
                  pl.BlockSpec((None, d, tn), lambda i, j: (b_layer, 0, j))],
        out_specs=[out_spec] * 3 + [pl.BlockSpec((rows, tn), lambda i, j: (i, 0))],
        out_shape=[jax.ShapeDtypeStruct(shape, F32), jax.ShapeDtypeStruct(shape, F32),
                   jax.ShapeDtypeStruct(shape, BF16), jax.ShapeDtypeStruct((m, tn), BF16)],
        compiler_params=_params(2),
        name="inproj_b",
    )(xb, w_kv_t, w_kv_t, w_in_b_t)


def _softplus(z):
    return jnp.maximum(z, 0.0) + jnp.log1p(jnp.exp(-jnp.abs(z)))


def _sb_block(q_ref, k_ref, v_ref, ones_tri, carried_scr, out_scr, *, newest):
    n_heads, qb, _ = q_ref.shape
    kb = k_ref.shape[1]
    scale = 1.0 / math.sqrt(HEAD_DIM)
    z = jnp.concatenate(
        [lax.dot_general(q_ref[h], k_ref[h].astype(BF16), _NT, preferred_element_type=F32)
         for h in range(n_heads)], axis=0) * scale
    sp = _softplus(z)
    log_a = z - sp
    if newest:
        row = lax.broadcasted_iota(jnp.int32, z.shape, 0) & (qb - 1)
        col = lax.broadcasted_iota(jnp.int32, z.shape, 1)
        valid = col < row
        sp = jnp.where(valid, sp, 0.0)
    hi = sp.astype(BF16)
    lo = (sp - hi.astype(F32)).astype(BF16)
    sums = _dot(hi, ones_tri) + _dot(lo, ones_tri)
    log_a = log_a - sums[:, HEAD_DIM:HEAD_DIM + kb]
    if not newest:
        log_a = log_a - carried_scr[:, :kb]
    a = jnp.exp(log_a)
    if newest:
        a = jnp.where(valid, a, 0.0)
    a = a.astype(BF16)
    for h in range(n_heads):
        contrib = _dot(a[h * qb:(h + 1) * qb], v_ref[h].astype(BF16))
        if newest:
            out_scr[h] = contrib
        else:
            out_scr[h] += contrib
    if newest:
        carried_scr[...] = sums[:, :HEAD_DIM]
    else:
        carried_scr[...] += sums[:, :HEAD_DIM]


def _sb_kernel(q_ref, kd_ref, vd_ref, kp_ref, vp_ref, otd_ref, otp_ref, k_hbm, v_hbm, o_ref,
               carried_scr, out_scr, k_buf, v_buf, sem, done_ref, *, n_steps, n_past_static):
    b = pl.program_id(0)
    j = pl.program_id(2)
    n_past = pl.program_id(1) if n_past_static is None else n_past_static

    def all_underflowed():
        return (jnp.min(carried_scr[...]) > F32_EXP_UNDERFLOW).astype(jnp.int32)

    @pl.when(j == 0)
    def _():
        _sb_block(q_ref, kd_ref, vd_ref, otd_ref[...], carried_scr, out_scr, newest=True)
        done_ref[0] = all_underflowed()

    @pl.when((j >= 1) & (j <= n_past) & (done_ref[0] == 0))
    def _():
        _sb_block(q_ref, kp_ref, vp_ref, otp_ref[...], carried_scr, out_scr, newest=False)
        done_ref[0] = all_underflowed()

    @pl.when(j == n_steps - 1)
    def _():
        n_left = n_past - (n_steps - 1)

        def fetch(hbm, buf, slot, blk):
            start = pl.multiple_of(blk * K_BLOCK, K_BLOCK)
            return pltpu.make_async_copy(hbm.at[b, :, pl.ds(start, K_BLOCK), :], buf, sem.at[slot])

        def body(carry):
            i, _ = carry
            blk = n_left - 1 - i
            fetch(k_hbm, k_buf, 0, blk).start()
            fetch(v_hbm, v_buf, 1, blk).start()
            fetch(k_hbm, k_buf, 0, blk).wait()
            fetch(v_hbm, v_buf, 1, blk).wait()
            _sb_block(q_ref, k_buf, v_buf, otp_ref[...], carried_scr, out_scr, newest=False)
            return i + 1, all_underflowed()

        lax.while_loop(lambda c: (c[0] < n_left) & (c[1] == 0), body, (jnp.int32(0), done_ref[0]))
        for h in range(out_scr.shape[0]):
            o_ref[:, h * HEAD_DIM:(h + 1) * HEAD_DIM] = out_scr[h].astype(o_ref.dtype)


def _ones_tri(kb):
    tri = np.tril(np.ones((kb, kb), np.float32), -1)
    return jnp.asarray(np.concatenate([np.ones((kb, HEAD_DIM), np.float32), tri], axis=1), BF16)


def _stick_breaking(q, k_new, v_new, k_past, v_past):
    n_seq, n_heads, seq_len, _ = q.shape
    c = n_heads * HEAD_DIM
    qb = min(K_BLOCK, seq_len)
    assert qb & (qb - 1) == 0
    nq = seq_len // qb
    if k_past is None:
        assert qb == K_BLOCK
        k_past, v_past, past_blocks, n_past_static = k_new, v_new, nq, None
        n_steps = min(SB_WINDOW_STEPS, nq)
    else:
        assert nq == 1 and k_past.shape[2] % K_BLOCK == 0
        past_blocks = k_past.shape[2] // K_BLOCK
        n_past_static = past_blocks
        n_steps = min(SB_WINDOW_STEPS, past_blocks + 1)

    def past_map(b, i, j):
        n_past = i if n_past_static is None else n_past_static
        return (b, 0, jnp.clip(n_past - j, 0, past_blocks - 1), 0)

    new_spec = pl.BlockSpec((None, n_heads, qb, HEAD_DIM), lambda b, i, j: (b, 0, i, 0))
    past_spec = pl.BlockSpec((None, n_heads, K_BLOCK, HEAD_DIM), past_map)
    hbm_spec = pl.BlockSpec(memory_space=pl.ANY)
    otd, otp = _ones_tri(qb), _ones_tri(K_BLOCK)
    return pl.pallas_call(
        functools.partial(_sb_kernel, n_steps=n_steps, n_past_static=n_past_static),
        grid=(n_seq, nq, n_steps),
        in_specs=[new_spec, new_spec, new_spec, past_spec, past_spec,
                  pl.BlockSpec(otd.shape, lambda b, i, j: (0, 0)),
                  pl.BlockSpec(otp.shape, lambda b, i, j: (0, 0)),
                  hbm_spec, hbm_spec],
        out_specs=pl.BlockSpec((qb, c), lambda b, i, j: (b * nq + i, 0)),
        out_shape=jax.ShapeDtypeStruct((n_seq * seq_len, c), BF16),
        scratch_shapes=[pltpu.VMEM((n_heads * qb, HEAD_DIM), F32),
                        pltpu.VMEM((n_heads, qb, HEAD_DIM), F32),
                        pltpu.VMEM((n_heads, K_BLOCK, HEAD_DIM), k_past.dtype),
                        pltpu.VMEM((n_heads, K_BLOCK, HEAD_DIM), v_past.dtype),
                        pltpu.SemaphoreType.DMA((2,)),
                        pltpu.SMEM((1,), jnp.int32)],
        compiler_params=_params(3),
        name="stick_breaking",
    )(q, k_new, v_new, k_past, v_past, otd, otp, k_past, v_past)


def _trunk(x, conv_state, k_past, v_past, mem_k, mem_v, w, *, attn_rows):
    n_seq, seq_len, d = x.shape
    depth, n_a, d_main, alpha = w["depth"], w["n_a"], w["d_main"], w["alpha"]
    xf = x.reshape(n_seq * seq_len, d)
    xb = xf
    conv_states = []
    k_new = v_new = None
    for layer in range(depth):
        if layer < n_a:
            main, st, q_mem = _inproj_a(xb, conv_state, w["w_in_a"], w["conv_w_t"], layer,
                                        n_seq, seq_len)
            conv_states.append(st)
        else:
            k, v, q, q_mem = _inproj_b(xb, w["w_kv"], w["w_in_b"], layer - n_a, d_main,
                                       n_seq, seq_len)
            if k_new is None:
                k_new, v_new = k, v
            kp = None if k_past is None else jnp.transpose(k_past, (0, 2, 1, 3))
            vp = None if v_past is None else jnp.transpose(v_past, (0, 2, 1, 3))
            main = _stick_breaking(q, k_new, v_new, kp, vp)
        mo = _mem_attn(q_mem, mem_k, mem_v, layer, n_seq, seq_len, rows=attn_rows)
        xf, xb = _out_proj(xf, main, mo, w["w_out"], w["ln_g"], w["ln_b"], layer, alpha)
        xf, xb = _mlp(xb, xf, w["w_ff1"], w["w_ff2"], w["ln_g"], w["ln_b"], layer, alpha,
                      want_bf16=layer + 1 < depth)
    return (xf.reshape(n_seq, seq_len, d), jnp.stack(conv_states),
            jnp.transpose(k_new, (0, 2, 1, 3)), jnp.transpose(v_new, (0, 2, 1, 3)))


def kernel(x_prompt, x_sample, state_conv, cache_k, cache_v, cache_mem_k, cache_mem_v, mem_prompt,
           w_in_a, conv_w, w_in_b, w_kv, w_mem_kv, w_out, w_ff1, w_ff2, ln_g, ln_b):
    depth, d_mix, d = w_out.shape
    n_a, d_main = conv_w.shape[0], conv_w.shape[1]
    d_mem = d_mix - d_main
    n_mem_heads = d_mem // HEAD_DIM
    n_prompt, n_mem = mem_prompt.shape[0], mem_prompt.shape[1]
    w = dict(
        depth=depth, n_a=n_a, d_main=d_main, alpha=(2.0 * depth) ** 0.25,
        w_in_a=w_in_a.astype(BF16), conv_w_t=jnp.swapaxes(conv_w, 1, 2),
        w_in_b=w_in_b.astype(BF16), w_kv=w_kv.astype(BF16), w_out=w_out.astype(BF16),
        w_ff1=w_ff1.astype(BF16), w_ff2=w_ff2.astype(BF16),
        ln_g=ln_g.reshape(depth * 2, 1, d), ln_b=ln_b.reshape(depth * 2, 1, d))

    mem_k_p, mem_v_p = _memkv(mem_prompt.reshape(n_prompt * n_mem, d), w_mem_kv.astype(BF16))
    rows_view = lambda a, b: a.reshape(depth, b, n_mem * n_mem_heads, HEAD_DIM)
    mem_k_p, mem_v_p = rows_view(mem_k_p, n_prompt), rows_view(mem_v_p, n_prompt)
    y_p, conv_p, k_p, v_p = _trunk(x_prompt, None, None, None, mem_k_p, mem_v_p, w, attn_rows=1024)
    n_sample = x_sample.shape[0]
    y_s, conv_s, k_s, v_s = _trunk(x_sample, state_conv, cache_k, cache_v,
                                   rows_view(cache_mem_k, n_sample), rows_view(cache_mem_v, n_sample),
                                   w, attn_rows=512)
    mem_shape = (depth, n_prompt, n_mem, n_mem_heads, HEAD_DIM)
    return (y_p, y_s, conv_p, conv_s, k_p, v_p, k_s, v_s,
            mem_k_p.reshape(mem_shape), mem_v_p.reshape(mem_shape))
```

```python
import functools
import math

import jax
import jax.numpy as jnp
import numpy as np
from jax import lax
from jax.experimental import pallas as pl
from jax.experimental.pallas import tpu as pltpu

F32 = jnp.float32
BF16 = jnp.bfloat16

HEAD_DIM = 128
LN_EPS = 1e-5
K_BLOCK = 128
SB_TAIL_ROWS = 32
HALO_ROWS = 16
F32_EXP_UNDERFLOW = 104.0
WEIGHT_TILE = 512
VMEM_LIMIT_BYTES = 48 * 1024 * 1024
MLP_VMEM_LIMIT_BYTES = 58 * 1024 * 1024

_NT = (((1,), (1,)), ((), ()))


def _dot(a, b):
    return jnp.dot(a, b, preferred_element_type=F32)


def _params(grid_rank, vmem_limit_bytes=VMEM_LIMIT_BYTES):
    return pltpu.CompilerParams(dimension_semantics=("arbitrary",) * grid_rank,
                                vmem_limit_bytes=vmem_limit_bytes)


def _layer_norm(y, g, b):
    mu = jnp.mean(y, axis=-1, keepdims=True)
    d = y - mu
    var = jnp.mean(d * d, axis=-1, keepdims=True)
    return d * lax.rsqrt(var + LN_EPS) * g + b


def _seq_tiling(n_seq, seq_len, rows):
    if seq_len >= rows:
        assert seq_len % rows == 0
        return 1, rows, seq_len // rows
    assert rows % seq_len == 0 and n_seq % (rows // seq_len) == 0
    return rows // seq_len, seq_len, 1


def _memkv_kernel(mem_ref, w_ref, k_ref, v_ref, *, n_heads):
    rows = mem_ref.shape[0]
    r = _dot(mem_ref[...].astype(BF16), w_ref[...])
    for i, ref in enumerate((k_ref, v_ref)):
        for h in range(n_heads):
            col = (i * n_heads + h) * HEAD_DIM
            ref[pl.ds(h, rows, stride=n_heads), :] = r[:, col:col + HEAD_DIM]


def _memkv(mem, w_mem_kv_b):
    rows, d = mem.shape
    depth, _, two_dmem = w_mem_kv_b.shape
    n_heads = two_dmem // 2 // HEAD_DIM
    out = jax.ShapeDtypeStruct((depth, rows * n_heads, HEAD_DIM), F32)
    return pl.pallas_call(
        functools.partial(_memkv_kernel, n_heads=n_heads),
        grid=(depth,),
        in_specs=[pl.BlockSpec((rows, d), lambda l: (0, 0)),
                  pl.BlockSpec((None, d, two_dmem), lambda l: (l, 0, 0))],
        out_specs=[pl.BlockSpec((None, rows * n_heads, HEAD_DIM), lambda l: (l, 0, 0))] * 2,
        out_shape=[out, out],
        compiler_params=_params(1),
        name="memkv",
    )(mem, w_mem_kv_b)


def _inproj_a_kernel(x_ref, halo_ref, wx_ref, wb_ref, wc_ref, cw_ref, wqm_ref,
                     main_ref, st_ref, qm_ref, u_scr, *, S, L, tiles_per_seq, has_state):
    j = pl.program_id(1)

    @pl.when(j > 0)
    def _():
        x = x_ref[...].astype(BF16)
        wx = wx_ref[...]
        wc = wc_ref[...]
        xin = _dot(x, wx)
        gate_b = _dot(x, wb_ref[...])
        gate_c = _dot(x, wc)
        u = gate_c * xin
        cw = cw_ref[...]
        w0, w1, w2 = cw[0:1], cw[1:2], cw[2:3]
        if not has_state:
            xh = halo_ref[...].astype(BF16)
            uh = _dot(xh, wc) * _dot(xh, wx)
            first = (pl.program_id(0) % tiles_per_seq) == 0
            u_scr[0:HALO_ROWS, :] = jnp.where(first, 0.0, uh)
        for s in range(S):
            rows = slice(s * L, (s + 1) * L)
            if has_state:
                u_scr[HALO_ROWS - 2:HALO_ROWS, :] = halo_ref[s]
            u_scr[HALO_ROWS:HALO_ROWS + L, :] = u[rows]
            y = u_scr[HALO_ROWS - 2:HALO_ROWS - 2 + L, :] * w0
            y = y + u_scr[HALO_ROWS - 1:HALO_ROWS - 1 + L, :] * w1
            y = y + u[rows] * w2
            main_ref[rows, :] = (gate_b[rows] * y).astype(main_ref.dtype)
            st_ref[s] = u_scr[HALO_ROWS + L - 2:HALO_ROWS + L, :]

    @pl.when(j == 0)
    def _():
        qm_ref[...] = _dot(x_ref[...].astype(BF16), wqm_ref[...]).astype(BF16)


def _inproj_a(x, state, w_in_a_t, conv_w_t, layer, n_seq, seq_len, *, rows=1024):
    m, d = x.shape
    c = conv_w_t.shape[-1]
    tn = WEIGHT_TILE
    S, L, tps = _seq_tiling(n_seq, seq_len, rows)
    nct = c // tn
    has_state = state is not None
    cj = lambda j: jnp.maximum(j - 1, 0)
    if has_state:
        assert tps == 1
        halo = state
        halo_spec = pl.BlockSpec((None, S, 2, tn), lambda i, j: (layer, i, 0, cj(j)))
    else:
        halo = x
        halo_spec = pl.BlockSpec(
            (HALO_ROWS, d), lambda i, j: (jnp.maximum(i * (rows // HALO_ROWS) - 1, 0), 0))
    kern = functools.partial(_inproj_a_kernel, S=S, L=L, tiles_per_seq=tps, has_state=has_state)
    w_spec = lambda off: pl.BlockSpec((None, d, tn), lambda i, j: (layer, 0, off + cj(j)))
    main, tile_state, q_mem = pl.pallas_call(
        kern,
        grid=(m // rows, nct + 1),
        in_specs=[pl.BlockSpec((rows, d), lambda i, j: (i, 0)),
                  halo_spec,
                  w_spec(0), w_spec(nct), w_spec(2 * nct),
                  pl.BlockSpec((None, 3, tn), lambda i, j: (layer, 0, cj(j))),
                  pl.BlockSpec((None, d, tn), lambda i, j: (layer, 0, 3 * nct))],
        out_specs=[pl.BlockSpec((rows, tn), lambda i, j: (i, cj(j))),
                   pl.BlockSpec((S, 2, tn), lambda i, j: (i, 0, cj(j))),
                   pl.BlockSpec((rows, tn), lambda i, j: (i, 0))],
        out_shape=[jax.ShapeDtypeStruct((m, c), BF16),
                   jax.ShapeDtypeStruct((n_seq * tps, 2, c), F32),
                   jax.ShapeDtypeStruct((m, tn), BF16)],
        scratch_shapes=[pltpu.VMEM((HALO_ROWS + L, tn), F32)],
        compiler_params=_params(2),
        name="inproj_a",
    )(x, halo, w_in_a_t, w_in_a_t, w_in_a_t, conv_w_t, w_in_a_t)
    return main, tile_state.reshape(n_seq, tps, 2, c)[:, tps - 1], q_mem


def _mem_attn_kernel(q_ref, mk_ref, mv_ref, o_ref, *, S, L):
    n_heads = q_ref.shape[1] // HEAD_DIM
    n_mem = mk_ref.shape[1] // n_heads
    scale = 1.0 / math.sqrt(HEAD_DIM)
    tiles = [(slice(s * L, (s + 1) * L), slice(h * HEAD_DIM, (h + 1) * HEAD_DIM),
              s, pl.ds(h, n_mem, stride=n_heads))
             for s in range(S) for h in range(n_heads)]
    sc = jnp.concatenate(
        [lax.dot_general(q_ref[rows, cols], mk_ref[s, head_rows, :].astype(BF16), _NT,
                         preferred_element_type=F32) for rows, cols, s, head_rows in tiles],
        axis=0) * scale
    e = jnp.exp(sc - jnp.max(sc, axis=-1, keepdims=True))
    p = (e / jnp.sum(e, axis=-1, keepdims=True)).astype(BF16)
    for i, (rows, cols, s, head_rows) in enumerate(tiles):
        v = mv_ref[s, head_rows, :].astype(BF16)
        o_ref[rows, cols] = _dot(p[i * L:(i + 1) * L], v).astype(BF16)


def _mem_attn(q_mem, mem_k, mem_v, layer, n_seq, seq_len, *, rows):
    m, d_mem = q_mem.shape
    S, L, tps = _seq_tiling(n_seq, seq_len, rows)
    mem_spec = pl.BlockSpec((None, S) + mem_k.shape[2:], lambda i: (layer, i // tps, 0, 0))
    row_spec = pl.BlockSpec((rows, d_mem), lambda i: (i, 0))
    return pl.pallas_call(
        functools.partial(_mem_attn_kernel, S=S, L=L),
        grid=(m // rows,),
        in_specs=[row_spec, mem_spec, mem_spec],
        out_specs=row_spec,
        out_shape=jax.ShapeDtypeStruct((m, d_mem), BF16),
        compiler_params=_params(1),
        name="mem_attn",
    )(q_mem, mem_k, mem_v)


def _out_proj_kernel(x_ref, main_ref, mo_ref, wo_ref, g_ref, b_ref, yf_ref, yb_ref, *, alpha):
    d_main = main_ref.shape[1]
    mix = _dot(main_ref[...], wo_ref[0:d_main, :]) + _dot(mo_ref[...], wo_ref[d_main:, :])
    y = _layer_norm(alpha * x_ref[...] + mix, g_ref[...], b_ref[...])
    yf_ref[...] = y
    yb_ref[...] = y.astype(BF16)


def _out_proj(xf, main, mo, w_out_b, ln_g, ln_b, layer, alpha, *, rows=512):
    m, d = xf.shape
    d_main, d_mem = main.shape[1], mo.shape[1]
    ln_spec = pl.BlockSpec((None, 1, d), lambda i: (2 * layer, 0, 0))
    return pl.pallas_call(
        functools.partial(_out_proj_kernel, alpha=alpha),
        grid=(m // rows,),
        in_specs=[pl.BlockSpec((rows, d), lambda i: (i, 0)),
                  pl.BlockSpec((rows, d_main), lambda i: (i, 0)),
                  pl.BlockSpec((rows, d_mem), lambda i: (i, 0)),
                  pl.BlockSpec((None, d_main + d_mem, d), lambda i: (layer, 0, 0),
                               pipeline_mode=pl.Buffered(1)),
                  ln_spec, ln_spec],
        out_specs=[pl.BlockSpec((rows, d), lambda i: (i, 0))] * 2,
        out_shape=[jax.ShapeDtypeStruct((m, d), F32), jax.ShapeDtypeStruct((m, d), BF16)],
        compiler_params=_params(1),
        name="out_proj",
    )(xf, main, mo, w_out_b, ln_g, ln_b)


def _mlp_kernel(xb_ref, xf_ref, w1_ref, w2_ref, g_ref, b_ref, yf_ref, *yb_ref, alpha):
    f = pl.program_id(1)

    @pl.when(f == 0)
    def _():
        yf_ref[...] = jnp.zeros_like(yf_ref)

    h = jnp.square(jnp.maximum(_dot(xb_ref[...], w1_ref[...]), 0.0)).astype(BF16)
    yf_ref[...] += _dot(h, w2_ref[...])

    @pl.when(f == pl.num_programs(1) - 1)
    def _():
        y = _layer_norm(alpha * xf_ref[...] + yf_ref[...], g_ref[...], b_ref[...])
        yf_ref[...] = y
        for ref in yb_ref:
            ref[...] = y.astype(BF16)


def _mlp(xb, xf, w_ff1_t, w_ff2_b, ln_g, ln_b, layer, alpha, *, want_bf16, rows=1024):
    m, d = xf.shape
    tf = WEIGHT_TILE
    n_f = w_ff1_t.shape[2] // tf
    ln_spec = pl.BlockSpec((None, 1, d), lambda i, f: (2 * layer + 1, 0, 0))
    row_spec = pl.BlockSpec((rows, d), lambda i, f: (i, 0))
    out_shape = [jax.ShapeDtypeStruct((m, d), F32)] + [jax.ShapeDtypeStruct((m, d), BF16)] * want_bf16
    out = pl.pallas_call(
        functools.partial(_mlp_kernel, alpha=alpha),
        grid=(m // rows, n_f),
        in_specs=[row_spec,
                  pl.BlockSpec((rows, d), lambda i, f: (i, 0), pipeline_mode=pl.Buffered(1)),
                  pl.BlockSpec((None, d, tf), lambda i, f: (layer, 0, f)),
                  pl.BlockSpec((None, tf, d), lambda i, f: (layer, f, 0)),
                  ln_spec, ln_spec],
        out_specs=[row_spec] * len(out_shape),
        out_shape=out_shape,
        compiler_params=_params(2, MLP_VMEM_LIMIT_BYTES),
        name="mlp",
    )(xb, xf, w_ff1_t, w_ff2_b, ln_g, ln_b)
    return out if want_bf16 else (out[0], None)


def _inproj_b_kernel(x_ref, wk_ref, wv_ref, wq_ref, k_ref, v_ref, q_ref, qm_ref, *, S, L):
    j = pl.program_id(1)

    @pl.when(j > 0)
    def _():
        x = x_ref[...]
        for w_ref, o_ref in ((wk_ref, k_ref), (wv_ref, v_ref), (wq_ref, q_ref)):
            r = _dot(x, w_ref[...])
            for s in range(S):
                for h in range(o_ref.shape[1]):
                    o_ref[s, h] = r[s * L:(s + 1) * L,
                                    h * HEAD_DIM:(h + 1) * HEAD_DIM].astype(o_ref.dtype)

    @pl.when(j == 0)
    def _():
        qm_ref[...] = _dot(x_ref[...], wq_ref[...]).astype(BF16)


def _inproj_b(xb, w_kv_t, w_in_b_t, b_layer, d_main, n_seq, seq_len, *, rows=1024):
    m, d = xb.shape
    n_heads = d_main // HEAD_DIM
    tn = WEIGHT_TILE
    nct = d_main // tn
    assert w_in_b_t.shape[-1] == (nct + 1) * tn
    S, L, tps = _seq_tiling(n_seq, seq_len, rows)
    cj = lambda j: jnp.maximum(j - 1, 0)
    out_spec = pl.BlockSpec((S, tn // HEAD_DIM, L, HEAD_DIM),
                            lambda i, j: (i // tps, cj(j), i % tps, 0))
    shape = (n_seq, n_heads, seq_len, HEAD_DIM)
    return pl.pallas_call(
        functools.partial(_inproj_b_kernel, S=S, L=L),
        grid=(m // rows, nct + 1),
        in_specs=[pl.BlockSpec((rows, d), lambda i, j: (i, 0)),
                  pl.BlockSpec((d, tn), lambda i, j: (0, cj(j))),
                  pl.BlockSpec((d, tn), lambda i, j: (0, nct + cj(j))),
                  pl.BlockSpec((None, d, tn),
                               lambda i, j: (b_layer, 0, jnp.where(j == 0, nct, j - 1)))],
        out_specs=[out_spec] * 3 + [pl.BlockSpec((rows, tn), lambda i, j: (i, 0))],
        out_shape=[jax.ShapeDtypeStruct(shape, F32), jax.ShapeDtypeStruct(shape, F32),
                   jax.ShapeDtypeStruct(shape, BF16), jax.ShapeDtypeStruct((m, tn), BF16)],
        compiler_params=_params(2),
        name="inproj_b",
    )(xb, w_kv_t, w_kv_t, w_in_b_t)


def _softplus(z):
    return jnp.maximum(z, 0.0) + jnp.log(1.0 + jnp.exp(-jnp.abs(z)))


def _sb_block(q_ref, k_ref, v_ref, ones_tri, carried_scr, out_scr, *, newest, rows=None):
    n_heads, qb, _ = q_ref.shape
    rows = qb if rows is None else rows
    kb = k_ref.shape[1]
    scale = 1.0 / math.sqrt(HEAD_DIM)
    stack = lambda per_head: jnp.concatenate([per_head(h) for h in range(n_heads)], axis=0)
    z = stack(lambda h: lax.dot_general(q_ref[h, :rows], k_ref[h].astype(BF16), _NT,
                                        preferred_element_type=F32)) * scale
    sp = _softplus(z)
    log_a = z - sp
    if newest:
        row = lax.broadcasted_iota(jnp.int32, z.shape, 0) & (qb - 1)
        col = lax.broadcasted_iota(jnp.int32, z.shape, 1)
        valid = col < row
        sp = jnp.where(valid, sp, 0.0)
    hi = sp.astype(BF16)
    lo = (sp - hi.astype(F32)).astype(BF16)
    if ones_tri.shape[0] == 2 * kb:
        sums = _dot(jnp.concatenate([hi, lo], axis=1), ones_tri)
    else:
        sums = _dot(hi, ones_tri) + _dot(lo, ones_tri)
    log_a = log_a - sums[:, HEAD_DIM:HEAD_DIM + kb]
    if not newest:
        log_a = log_a - stack(lambda h: carried_scr[h, :rows, :kb])
    a = jnp.exp(log_a)
    if newest:
        a = jnp.where(valid, a, 0.0)
    a = a.astype(BF16)
    for h in range(n_heads):
        head = slice(h * rows, (h + 1) * rows)
        contrib = _dot(a[head], v_ref[h].astype(BF16))
        if newest:
            out_scr[h] = contrib
            carried_scr[h] = sums[head, :HEAD_DIM]
        else:
            out_scr[h, :rows] += contrib
            carried_scr[h, :rows] += sums[head, :HEAD_DIM]


def _sb_kernel(q_ref, kd_ref, vd_ref, k1_ref, v1_ref, k2_ref, v2_ref, otd_ref, otp_ref,
               k_hbm, v_hbm, o_ref, carried_scr, out_scr, k_buf, v_buf, sem, *, n_past_static):
    b = pl.program_id(0)
    n_past = pl.program_id(1) if n_past_static is None else n_past_static
    qb = q_ref.shape[1]
    tail = min(SB_TAIL_ROWS, qb)

    def alive(first_row):
        return jnp.min(carried_scr[:, first_row:, :]) <= F32_EXP_UNDERFLOW

    def past_block(k_ref, v_ref, rows=None):
        _sb_block(q_ref, k_ref, v_ref, otp_ref[...], carried_scr, out_scr, newest=False, rows=rows)

    _sb_block(q_ref, kd_ref, vd_ref, otd_ref[...], carried_scr, out_scr, newest=True)

    @pl.when(n_past >= 1)
    def _():
        past_block(k1_ref, v1_ref)

    any_alive, tail_alive = alive(0), alive(tail)

    @pl.when((n_past >= 2) & tail_alive)
    def _():
        past_block(k2_ref, v2_ref)

    @pl.when((n_past >= 2) & any_alive & jnp.logical_not(tail_alive))
    def _():
        past_block(k2_ref, v2_ref, rows=tail)

    n_left = n_past - 2

    def fetch(hbm, buf, slot, blk):
        start = pl.multiple_of(blk * K_BLOCK, K_BLOCK)
        return pltpu.make_async_copy(hbm.at[b, :, pl.ds(start, K_BLOCK), :], buf, sem.at[slot])

    def body(carry):
        i, _ = carry
        blk = n_left - 1 - i
        fetch(k_hbm, k_buf, 0, blk).start()
        fetch(v_hbm, v_buf, 1, blk).start()
        fetch(k_hbm, k_buf, 0, blk).wait()
        fetch(v_hbm, v_buf, 1, blk).wait()
        past_block(k_buf, v_buf)
        return i + 1, alive(0)

    @pl.when(n_left > 0)
    def _():
        lax.while_loop(lambda c: (c[0] < n_left) & c[1], body, (jnp.int32(0), alive(0)))

    for h in range(out_scr.shape[0]):
        o_ref[:, h * HEAD_DIM:(h + 1) * HEAD_DIM] = out_scr[h].astype(o_ref.dtype)


def _ones_tri(kb):
    tri = np.tril(np.ones((kb, kb), np.float32), -1)
    m = np.concatenate([np.ones((kb, HEAD_DIM), np.float32), tri], axis=1)
    if kb % 128 == 0:
        m = np.concatenate([m, m], axis=0)
    return jnp.asarray(m, BF16)


def _stick_breaking(q, k_new, v_new, k_past, v_past):
    n_seq, n_heads, seq_len, _ = q.shape
    c = n_heads * HEAD_DIM
    qb = min(K_BLOCK, seq_len)
    assert qb & (qb - 1) == 0
    nq = seq_len // qb
    if k_past is None:
        assert qb == K_BLOCK
        k_past, v_past, past_blocks, n_past_static = k_new, v_new, nq, None
    else:
        assert nq == 1 and k_past.shape[2] % K_BLOCK == 0
        past_blocks = k_past.shape[2] // K_BLOCK
        n_past_static = past_blocks

    def past_spec(back):
        def index_map(b, i):
            n_past = i if n_past_static is None else n_past_static
            return (b, 0, jnp.clip(n_past - back, 0, past_blocks - 1), 0)
        return pl.BlockSpec((None, n_heads, K_BLOCK, HEAD_DIM), index_map)

    new_spec = pl.BlockSpec((None, n_heads, qb, HEAD_DIM), lambda b, i: (b, 0, i, 0))
    hbm_spec = pl.BlockSpec(memory_space=pl.ANY)
    otd, otp = _ones_tri(qb), _ones_tri(K_BLOCK)
    return pl.pallas_call(
        functools.partial(_sb_kernel, n_past_static=n_past_static),
        grid=(n_seq, nq),
        in_specs=[new_spec, new_spec, new_spec,
                  past_spec(1), past_spec(1), past_spec(2), past_spec(2),
                  pl.BlockSpec(otd.shape, lambda b, i: (0, 0)),
                  pl.BlockSpec(otp.shape, lambda b, i: (0, 0)),
                  hbm_spec, hbm_spec],
        out_specs=pl.BlockSpec((qb, c), lambda b, i: (b * nq + i, 0)),
        out_shape=jax.ShapeDtypeStruct((n_seq * seq_len, c), BF16),
        scratch_shapes=[pltpu.VMEM((n_heads, qb, HEAD_DIM), F32),
                        pltpu.VMEM((n_heads, qb, HEAD_DIM), F32),
                        pltpu.VMEM((n_heads, K_BLOCK, HEAD_DIM), k_past.dtype),
                        pltpu.VMEM((n_heads, K_BLOCK, HEAD_DIM), v_past.dtype),
                        pltpu.SemaphoreType.DMA((2,))],
        compiler_params=_params(2),
        name="stick_breaking",
    )(q, k_new, v_new, k_past, v_past, k_past, v_past, otd, otp, k_past, v_past)


def _trunk(x, conv_state, k_past, v_past, mem_k, mem_v, w, *, attn_rows):
    n_seq, seq_len, d = x.shape
    depth, n_a, d_main, alpha = w["depth"], w["n_a"], w["d_main"], w["alpha"]
    xf = x.reshape(n_seq * seq_len, d)
    xb = xf
    conv_states = []
    k_new = v_new = None
    for layer in range(depth):
        if layer < n_a:
            main, st, q_mem = _inproj_a(xb, conv_state, w["w_in_a"], w["conv_w_t"], layer,
                                        n_seq, seq_len)
            conv_states.append(st)
        else:
            k, v, q, q_mem = _inproj_b(xb, w["w_kv"], w["w_in_b"], layer - n_a, d_main,
                                       n_seq, seq_len)
            if k_new is None:
                k_new, v_new = k, v
            kp = None if k_past is None else jnp.transpose(k_past, (0, 2, 1, 3))
            vp = None if v_past is None else jnp.transpose(v_past, (0, 2, 1, 3))
            main = _stick_breaking(q, k_new, v_new, kp, vp)
        mo = _mem_attn(q_mem, mem_k, mem_v, layer, n_seq, seq_len, rows=attn_rows)
        xf, xb = _out_proj(xf, main, mo, w["w_out"], w["ln_g"], w["ln_b"], layer, alpha)
        xf, xb = _mlp(xb, xf, w["w_ff1"], w["w_ff2"], w["ln_g"], w["ln_b"], layer, alpha,
                      want_bf16=layer + 1 < depth)
    return (xf.reshape(n_seq, seq_len, d), jnp.stack(conv_states),
            jnp.transpose(k_new, (0, 2, 1, 3)), jnp.transpose(v_new, (0, 2, 1, 3)))


def kernel(x_prompt, x_sample, state_conv, cache_k, cache_v, cache_mem_k, cache_mem_v, mem_prompt,
           w_in_a, conv_w, w_in_b, w_kv, w_mem_kv, w_out, w_ff1, w_ff2, ln_g, ln_b):
    depth, d_mix, d = w_out.shape
    n_a, d_main = conv_w.shape[0], conv_w.shape[1]
    d_mem = d_mix - d_main
    n_mem_heads = d_mem // HEAD_DIM
    n_prompt, n_mem = mem_prompt.shape[0], mem_prompt.shape[1]
    w = dict(
        depth=depth, n_a=n_a, d_main=d_main, alpha=(2.0 * depth) ** 0.25,
        w_in_a=w_in_a.astype(BF16), conv_w_t=jnp.swapaxes(conv_w, 1, 2),
        w_in_b=w_in_b.astype(BF16), w_kv=w_kv.astype(BF16), w_out=w_out.astype(BF16),
        w_ff1=w_ff1.astype(BF16), w_ff2=w_ff2.astype(BF16),
        ln_g=ln_g.reshape(depth * 2, 1, d), ln_b=ln_b.reshape(depth * 2, 1, d))

    mem_k_p, mem_v_p = _memkv(mem_prompt.reshape(n_prompt * n_mem, d), w_mem_kv.astype(BF16))
    rows_view = lambda a, b: a.reshape(depth, b, n_mem * n_mem_heads, HEAD_DIM)
    mem_k_p, mem_v_p = rows_view(mem_k_p, n_prompt), rows_view(mem_v_p, n_prompt)
    y_p, conv_p, k_p, v_p = _trunk(x_prompt, None, None, None, mem_k_p, mem_v_p, w, attn_rows=1024)
    n_sample = x_sample.shape[0]
    y_s, conv_s, k_s, v_s = _trunk(x_sample, state_conv, cache_k, cache_v,
                                   rows_view(cache_mem_k, n_sample), rows_view(cache_mem_v, n_sample),
                                   w, attn_rows=512)
    mem_shape = (depth, n_prompt, n_mem, n_mem_heads, HEAD_DIM)
    return (y_p, y_s, conv_p, conv_s, k_p, v_p, k_s, v_s,
            mem_k_p.reshape(mem_shape), mem_v_p.reshape(mem_shape))
```

```python
import functools
import math

import jax
import jax.numpy as jnp
import numpy as np
from jax import lax
from jax.experimental import pallas as pl
from jax.experimental.pallas import tpu as pltpu

F32 = jnp.float32
BF16 = jnp.bfloat16

HEAD_DIM = 128
LN_EPS = 1e-5
K_BLOCK = 128
SB_TAIL_ROWS = 32
HALO_ROWS = 16
F32_EXP_UNDERFLOW = 104.0
WEIGHT_TILE = 512
EPILOGUE_ROWS = 128
CONV_CHUNK_ROWS = 256
MLP_EPILOGUE_ROWS = 256
VMEM_LIMIT_BYTES = 48 * 1024 * 1024
MLP_VMEM_LIMIT_BYTES = 58 * 1024 * 1024

_NT = (((1,), (1,)), ((), ()))


def _dot(a, b):
    return jnp.dot(a, b, preferred_element_type=F32)


def _params(grid_rank, vmem_limit_bytes=VMEM_LIMIT_BYTES):
    return pltpu.CompilerParams(dimension_semantics=("arbitrary",) * grid_rank,
                                vmem_limit_bytes=vmem_limit_bytes)


def _layer_norm(y, g, b):
    mu = jnp.mean(y, axis=-1, keepdims=True)
    d = y - mu
    var = jnp.mean(d * d, axis=-1, keepdims=True)
    return d * lax.rsqrt(var + LN_EPS) * g + b


def _seq_tiling(n_seq, seq_len, rows):
    if seq_len >= rows:
        assert seq_len % rows == 0
        return 1, rows, seq_len // rows
    assert rows % seq_len == 0 and n_seq % (rows // seq_len) == 0
    return rows // seq_len, seq_len, 1


def _memkv_kernel(mem_ref, w_ref, k_ref, v_ref, *, n_heads):
    rows = mem_ref.shape[0]
    r = _dot(mem_ref[...].astype(BF16), w_ref[...].astype(BF16))
    for i, ref in enumerate((k_ref, v_ref)):
        for h in range(n_heads):
            col = (i * n_heads + h) * HEAD_DIM
            ref[pl.ds(h, rows, stride=n_heads), :] = r[:, col:col + HEAD_DIM]


def _memkv(mem, w_mem_kv):
    rows, d = mem.shape
    depth, _, two_dmem = w_mem_kv.shape
    n_heads = two_dmem // 2 // HEAD_DIM
    out = jax.ShapeDtypeStruct((depth, rows * n_heads, HEAD_DIM), F32)
    return pl.pallas_call(
        functools.partial(_memkv_kernel, n_heads=n_heads),
        grid=(depth,),
        in_specs=[pl.BlockSpec((rows, d), lambda l: (0, 0)),
                  pl.BlockSpec((None, d, two_dmem), lambda l: (l, 0, 0))],
        out_specs=[pl.BlockSpec((None, rows * n_heads, HEAD_DIM), lambda l: (l, 0, 0))] * 2,
        out_shape=[out, out],
        compiler_params=_params(1),
        name="memkv",
    )(mem, w_mem_kv)


def _inproj_a_kernel(x_ref, halo_ref, wx_ref, wb_ref, wc_ref, cw_ref, wqm_ref,
                     main_ref, st_ref, qm_ref, u_scr, *, S, L, tiles_per_seq, has_state):
    j = pl.program_id(1)

    @pl.when(j > 0)
    def _():
        wx = wx_ref[...]
        wb = wb_ref[...]
        wc = wc_ref[...]
        cw = cw_ref[...]
        w0, w1, w2 = cw[0:1], cw[1:2], cw[2:3]
        if has_state:
            for s in range(S):
                u_scr[s, HALO_ROWS - 2:HALO_ROWS, :] = halo_ref[s]
        else:
            xh = halo_ref[...].astype(BF16)
            uh = _dot(xh, wc) * _dot(xh, wx)
            first = (pl.program_id(0) % tiles_per_seq) == 0
            u_scr[0, 0:HALO_ROWS, :] = jnp.where(first, 0.0, uh)
        piece = min(CONV_CHUNK_ROWS, L)
        assert CONV_CHUNK_ROWS % piece == 0 and L % piece == 0 and (S * L) % CONV_CHUNK_ROWS == 0
        for r in range(0, S * L, CONV_CHUNK_ROWS):
            x = x_ref[r:r + CONV_CHUNK_ROWS, :].astype(BF16)
            u = _dot(x, wc) * _dot(x, wx)
            gate_b = _dot(x, wb)
            for p in range(0, CONV_CHUNK_ROWS, piece):
                s, t0 = divmod(r + p, L)
                lo = HALO_ROWS + t0
                u_scr[s, lo:lo + piece, :] = u[p:p + piece]
                y = u_scr[s, lo - 2:lo - 2 + piece, :] * w0
                y = y + u_scr[s, lo - 1:lo - 1 + piece, :] * w1
                y = y + u[p:p + piece] * w2
                main_ref[r + p:r + p + piece, :] = (gate_b[p:p + piece] * y).astype(main_ref.dtype)
                if t0 + piece == L:
                    st_ref[s] = u_scr[s, HALO_ROWS + L - 2:HALO_ROWS + L, :]

    @pl.when(j == 0)
    def _():
        qm_ref[...] = _dot(x_ref[...].astype(BF16), wqm_ref[...]).astype(BF16)


def _inproj_a(x, state, w_in_a_t, conv_w_t, layer, n_seq, seq_len, *, rows=1024):
    m, d = x.shape
    c = conv_w_t.shape[-1]
    tn = WEIGHT_TILE
    S, L, tps = _seq_tiling(n_seq, seq_len, rows)
    nct = c // tn
    has_state = state is not None
    cj = lambda j: jnp.maximum(j - 1, 0)
    if has_state:
        assert tps == 1
        halo = state
        halo_spec = pl.BlockSpec((None, S, 2, tn), lambda i, j: (layer, i, 0, cj(j)))
    else:
        halo = x
        halo_spec = pl.BlockSpec(
            (HALO_ROWS, d), lambda i, j: (jnp.maximum(i * (rows // HALO_ROWS) - 1, 0), 0))
    kern = functools.partial(_inproj_a_kernel, S=S, L=L, tiles_per_seq=tps, has_state=has_state)
    w_spec = lambda off: pl.BlockSpec((None, d, tn), lambda i, j: (layer, 0, off + cj(j)))
    main, tile_state, q_mem = pl.pallas_call(
        kern,
        grid=(m // rows, nct + 1),
        in_specs=[pl.BlockSpec((rows, d), lambda i, j: (i, 0)),
                  halo_spec,
                  w_spec(0), w_spec(nct), w_spec(2 * nct),
                  pl.BlockSpec((None, 3, tn), lambda i, j: (layer, 0, cj(j))),
                  pl.BlockSpec((None, d, tn), lambda i, j: (layer, 0, 3 * nct))],
        out_specs=[pl.BlockSpec((rows, tn), lambda i, j: (i, cj(j))),
                   pl.BlockSpec((S, 2, tn), lambda i, j: (i, 0, cj(j))),
                   pl.BlockSpec((rows, tn), lambda i, j: (i, 0))],
        out_shape=[jax.ShapeDtypeStruct((m, c), BF16),
                   jax.ShapeDtypeStruct((n_seq * tps, 2, c), F32),
                   jax.ShapeDtypeStruct((m, tn), BF16)],
        scratch_shapes=[pltpu.VMEM((S, HALO_ROWS + L, tn), F32)],
        compiler_params=_params(2),
        name="inproj_a",
    )(x, halo, w_in_a_t, w_in_a_t, w_in_a_t, conv_w_t, w_in_a_t)
    return main, tile_state.reshape(n_seq, tps, 2, c)[:, tps - 1], q_mem


def _mem_attn_kernel(q_ref, mk_ref, mv_ref, o_ref, *, S, L):
    n_heads = q_ref.shape[1] // HEAD_DIM
    n_mem = mk_ref.shape[1] // n_heads
    scale = 1.0 / math.sqrt(HEAD_DIM)
    tiles = [(slice(s * L, (s + 1) * L), slice(h * HEAD_DIM, (h + 1) * HEAD_DIM),
              s, pl.ds(h, n_mem, stride=n_heads))
             for s in range(S) for h in range(n_heads)]
    sc = jnp.concatenate(
        [lax.dot_general(q_ref[rows, cols], mk_ref[s, head_rows, :].astype(BF16), _NT,
                         preferred_element_type=F32) for rows, cols, s, head_rows in tiles],
        axis=0) * scale
    e = jnp.exp(sc - jnp.max(sc, axis=-1, keepdims=True))
    p = (e / jnp.sum(e, axis=-1, keepdims=True)).astype(BF16)
    for i, (rows, cols, s, head_rows) in enumerate(tiles):
        v = mv_ref[s, head_rows, :].astype(BF16)
        o_ref[rows, cols] = _dot(p[i * L:(i + 1) * L], v).astype(BF16)


def _mem_attn(q_mem, mem_k, mem_v, layer, n_seq, seq_len, *, rows):
    m, d_mem = q_mem.shape
    S, L, tps = _seq_tiling(n_seq, seq_len, rows)
    mem_spec = pl.BlockSpec((None, S) + mem_k.shape[2:], lambda i: (layer, i // tps, 0, 0))
    row_spec = pl.BlockSpec((rows, d_mem), lambda i: (i, 0))
    return pl.pallas_call(
        functools.partial(_mem_attn_kernel, S=S, L=L),
        grid=(m // rows,),
        in_specs=[row_spec, mem_spec, mem_spec],
        out_specs=row_spec,
        out_shape=jax.ShapeDtypeStruct((m, d_mem), BF16),
        compiler_params=_params(1),
        name="mem_attn",
    )(q_mem, mem_k, mem_v)


def _out_proj_kernel(x_ref, main_ref, mo_ref, wo_ref, g_ref, b_ref, yf_ref, yb_ref, *, alpha):
    d_main = main_ref.shape[1]
    for r in range(0, x_ref.shape[0], EPILOGUE_ROWS):
        rows = slice(r, r + EPILOGUE_ROWS)
        mix = (_dot(main_ref[rows, :], wo_ref[0:d_main, :])
               + _dot(mo_ref[rows, :], wo_ref[d_main:, :]))
        y = _layer_norm(alpha * x_ref[rows, :] + mix, g_ref[...], b_ref[...])
        yf_ref[rows, :] = y
        yb_ref[rows, :] = y.astype(BF16)


def _out_proj(xf, main, mo, w_out_b, ln_g, ln_b, layer, alpha, *, rows=512):
    m, d = xf.shape
    d_main, d_mem = main.shape[1], mo.shape[1]
    ln_spec = pl.BlockSpec((None, 1, d), lambda i: (2 * layer, 0, 0))
    return pl.pallas_call(
        functools.partial(_out_proj_kernel, alpha=alpha),
        grid=(m // rows,),
        in_specs=[pl.BlockSpec((rows, d), lambda i: (i, 0)),
                  pl.BlockSpec((rows, d_main), lambda i: (i, 0)),
                  pl.BlockSpec((rows, d_mem), lambda i: (i, 0)),
                  pl.BlockSpec((None, d_main + d_mem, d), lambda i: (layer, 0, 0),
                               pipeline_mode=pl.Buffered(1)),
                  ln_spec, ln_spec],
        out_specs=[pl.BlockSpec((rows, d), lambda i: (i, 0))] * 2,
        out_shape=[jax.ShapeDtypeStruct((m, d), F32), jax.ShapeDtypeStruct((m, d), BF16)],
        compiler_params=_params(1),
        name="out_proj",
    )(xf, main, mo, w_out_b, ln_g, ln_b)


def _mlp_kernel(xb_ref, xf_ref, w1_ref, w2_ref, g_ref, b_ref, yf_ref, *yb_ref, alpha):
    f = pl.program_id(1)
    last = pl.num_programs(1) - 1

    def ff(rows):
        h = jnp.square(jnp.maximum(_dot(xb_ref[rows, :], w1_ref[...]), 0.0)).astype(BF16)
        return _dot(h, w2_ref[...])

    @pl.when(f == 0)
    def _():
        yf_ref[...] = ff(slice(None))

    @pl.when((f > 0) & (f < last))
    def _():
        yf_ref[...] += ff(slice(None))

    @pl.when(f == last)
    def _():
        for r in range(0, yf_ref.shape[0], MLP_EPILOGUE_ROWS):
            rows = slice(r, r + MLP_EPILOGUE_ROWS)
            acc = yf_ref[rows, :] + ff(rows)
            y = _layer_norm(alpha * xf_ref[rows, :] + acc, g_ref[...], b_ref[...])
            yf_ref[rows, :] = y
            for ref in yb_ref:
                ref[rows, :] = y.astype(BF16)


def _mlp(xb, xf, w_ff1_t, w_ff2_b, ln_g, ln_b, layer, alpha, *, want_bf16, rows=1024):
    m, d = xf.shape
    tf = WEIGHT_TILE
    n_f = w_ff1_t.shape[2] // tf
    ln_spec = pl.BlockSpec((None, 1, d), lambda i, f: (2 * layer + 1, 0, 0))
    row_spec = pl.BlockSpec((rows, d), lambda i, f: (i, 0))
    out_shape = [jax.ShapeDtypeStruct((m, d), F32)] + [jax.ShapeDtypeStruct((m, d), BF16)] * want_bf16
    out = pl.pallas_call(
        functools.partial(_mlp_kernel, alpha=alpha),
        grid=(m // rows, n_f),
        in_specs=[row_spec,
                  pl.BlockSpec((rows, d), lambda i, f: (i, 0), pipeline_mode=pl.Buffered(1)),
                  pl.BlockSpec((None, d, tf), lambda i, f: (layer, 0, f)),
                  pl.BlockSpec((None, tf, d), lambda i, f: (layer, f, 0)),
                  ln_spec, ln_spec],
        out_specs=[row_spec] * len(out_shape),
        out_shape=out_shape,
        compiler_params=_params(2, MLP_VMEM_LIMIT_BYTES),
        name="mlp",
    )(xb, xf, w_ff1_t, w_ff2_b, ln_g, ln_b)
    return out if want_bf16 else (out[0], None)


def _inproj_b_kernel(x_ref, wk_ref, wv_ref, wq_ref, k_ref, v_ref, q_ref, qm_ref, *, S, L):
    j = pl.program_id(1)

    @pl.when(j > 0)
    def _():
        x = x_ref[...]
        for w_ref, o_ref in ((wk_ref, k_ref), (wv_ref, v_ref), (wq_ref, q_ref)):
            r = _dot(x, w_ref[...])
            for s in range(S):
                for h in range(o_ref.shape[1]):
                    o_ref[s, h] = r[s * L:(s + 1) * L,
                                    h * HEAD_DIM:(h + 1) * HEAD_DIM].astype(o_ref.dtype)

    @pl.when(j == 0)
    def _():
        qm_ref[...] = _dot(x_ref[...], wq_ref[...]).astype(BF16)


def _inproj_b(xb, w_kv_t, w_in_b_t, b_layer, d_main, n_seq, seq_len, *, rows=1024):
    m, d = xb.shape
    n_heads = d_main // HEAD_DIM
    tn = WEIGHT_TILE
    nct = d_main // tn
    assert w_in_b_t.shape[-1] == (nct + 1) * tn
    S, L, tps = _seq_tiling(n_seq, seq_len, rows)
    cj = lambda j: jnp.maximum(j - 1, 0)
    out_spec = pl.BlockSpec((S, tn // HEAD_DIM, L, HEAD_DIM),
                            lambda i, j: (i // tps, cj(j), i % tps, 0))
    shape = (n_seq, n_heads, seq_len, HEAD_DIM)
    return pl.pallas_call(
        functools.partial(_inproj_b_kernel, S=S, L=L),
        grid=(m // rows, nct + 1),
        in_specs=[pl.BlockSpec((rows, d), lambda i, j: (i, 0)),
                  pl.BlockSpec((d, tn), lambda i, j: (0, cj(j))),
                  pl.BlockSpec((d, tn), lambda i, j: (0, nct + cj(j))),
                  pl.BlockSpec((None, d, tn),
                               lambda i, j: (b_layer, 0, jnp.where(j == 0, nct, j - 1)))],
        out_specs=[out_spec] * 3 + [pl.BlockSpec((rows, tn), lambda i, j: (i, 0))],
        out_shape=[jax.ShapeDtypeStruct(shape, F32), jax.ShapeDtypeStruct(shape, F32),
                   jax.ShapeDtypeStruct(shape, BF16), jax.ShapeDtypeStruct((m, tn), BF16)],
        compiler_params=_params(2),
        name="inproj_b",
    )(xb, w_kv_t, w_kv_t, w_in_b_t)


def _softplus(z):
    return jnp.maximum(z, 0.0) + jnp.log(1.0 + jnp.exp(-jnp.abs(z)))


def _sb_block(q_ref, k_ref, v_ref, ones_tri, carried_scr, out_scr, *, newest, rows=None):
    n_heads, qb, _ = q_ref.shape
    rows = qb if rows is None else rows
    kb = k_ref.shape[1]
    scale = 1.0 / math.sqrt(HEAD_DIM)
    stack = lambda per_head: jnp.concatenate([per_head(h) for h in range(n_heads)], axis=0)
    z = stack(lambda h: lax.dot_general(q_ref[h, :rows], k_ref[h].astype(BF16), _NT,
                                        preferred_element_type=F32)) * scale
    sp = _softplus(z)
    log_a = z - sp
    if newest:
        row = lax.broadcasted_iota(jnp.int32, z.shape, 0) & (qb - 1)
        col = lax.broadcasted_iota(jnp.int32, z.shape, 1)
        valid = col < row
        sp = jnp.where(valid, sp, 0.0)
    hi = sp.astype(BF16)
    lo = (sp - hi.astype(F32)).astype(BF16)
    if ones_tri.shape[0] == 2 * kb:
        sums = _dot(jnp.concatenate([hi, lo], axis=1), ones_tri)
    else:
        sums = _dot(hi, ones_tri) + _dot(lo, ones_tri)
    log_a = log_a - sums[:, HEAD_DIM:HEAD_DIM + kb]
    if not newest:
        log_a = log_a - stack(lambda h: carried_scr[h, :rows, :kb])
    a = jnp.exp(log_a)
    if newest:
        a = jnp.where(valid, a, 0.0)
    a = a.astype(BF16)
    for h in range(n_heads):
        head = slice(h * rows, (h + 1) * rows)
        contrib = _dot(a[head], v_ref[h].astype(BF16))
        if newest:
            out_scr[h] = contrib
            carried_scr[h] = sums[head, :HEAD_DIM]
        else:
            out_scr[h, :rows] += contrib
            carried_scr[h, :rows] += sums[head, :HEAD_DIM]


def _sb_kernel(q_ref, kd_ref, vd_ref, k1_ref, v1_ref, k2_ref, v2_ref, otd_ref, otp_ref,
               k_hbm, v_hbm, o_ref, carried_scr, out_scr, k_buf, v_buf, sem, *, n_past_static):
    b = pl.program_id(0)
    n_past = pl.program_id(1) if n_past_static is None else n_past_static
    qb = q_ref.shape[1]
    tail = min(SB_TAIL_ROWS, qb)

    def alive(first_row):
        return jnp.min(carried_scr[:, first_row:, :]) <= F32_EXP_UNDERFLOW

    def past_block(k_ref, v_ref, rows=None):
        _sb_block(q_ref, k_ref, v_ref, otp_ref[...], carried_scr, out_scr, newest=False, rows=rows)

    _sb_block(q_ref, kd_ref, vd_ref, otd_ref[...], carried_scr, out_scr, newest=True)

    @pl.when(n_past >= 1)
    def _():
        past_block(k1_ref, v1_ref)

    any_alive, tail_alive = alive(0), alive(tail)

    @pl.when((n_past >= 2) & tail_alive)
    def _():
        past_block(k2_ref, v2_ref)

    @pl.when((n_past >= 2) & any_alive & jnp.logical_not(tail_alive))
    def _():
        past_block(k2_ref, v2_ref, rows=tail)

    n_left = n_past - 2

    def fetch(hbm, buf, slot, blk):
        start = pl.multiple_of(blk * K_BLOCK, K_BLOCK)
        return pltpu.make_async_copy(hbm.at[b, :, pl.ds(start, K_BLOCK), :], buf, sem.at[slot])

    def body(carry):
        i, _ = carry
        blk = n_left - 1 - i
        fetch(k_hbm, k_buf, 0, blk).start()
        fetch(v_hbm, v_buf, 1, blk).start()
        fetch(k_hbm, k_buf, 0, blk).wait()
        fetch(v_hbm, v_buf, 1, blk).wait()
        past_block(k_buf, v_buf)
        return i + 1, alive(0)

    @pl.when(n_left > 0)
    def _():
        lax.while_loop(lambda c: (c[0] < n_left) & c[1], body, (jnp.int32(0), alive(0)))

    for h in range(out_scr.shape[0]):
        o_ref[:, h * HEAD_DIM:(h + 1) * HEAD_DIM] = out_scr[h].astype(o_ref.dtype)


def _ones_tri(kb):
    tri = np.tril(np.ones((kb, kb), np.float32), -1)
    m = np.concatenate([np.ones((kb, HEAD_DIM), np.float32), tri], axis=1)
    if kb % 128 == 0:
        m = np.concatenate([m, m], axis=0)
    return jnp.asarray(m, BF16)


def _stick_breaking(q, k_new, v_new, k_past, v_past):
    n_seq, n_heads, seq_len, _ = q.shape
    c = n_heads * HEAD_DIM
    qb = min(K_BLOCK, seq_len)
    assert qb & (qb - 1) == 0
    nq = seq_len // qb
    if k_past is None:
        assert qb == K_BLOCK
        k_past, v_past, past_blocks, n_past_static = k_new, v_new, nq, None
    else:
        assert nq == 1 and k_past.shape[2] % K_BLOCK == 0
        past_blocks = k_past.shape[2] // K_BLOCK
        n_past_static = past_blocks

    def past_spec(back):
        def index_map(b, i):
            n_past = i if n_past_static is None else n_past_static
            return (b, 0, jnp.clip(n_past - back, 0, past_blocks - 1), 0)
        return pl.BlockSpec((None, n_heads, K_BLOCK, HEAD_DIM), index_map)

    new_spec = pl.BlockSpec((None, n_heads, qb, HEAD_DIM), lambda b, i: (b, 0, i, 0))
    hbm_spec = pl.BlockSpec(memory_space=pl.ANY)
    otd, otp = _ones_tri(qb), _ones_tri(K_BLOCK)
    return pl.pallas_call(
        functools.partial(_sb_kernel, n_past_static=n_past_static),
        grid=(n_seq, nq),
        in_specs=[new_spec, new_spec, new_spec,
                  past_spec(1), past_spec(1), past_spec(2), past_spec(2),
                  pl.BlockSpec(otd.shape, lambda b, i: (0, 0)),
                  pl.BlockSpec(otp.shape, lambda b, i: (0, 0)),
                  hbm_spec, hbm_spec],
        out_specs=pl.BlockSpec((qb, c), lambda b, i: (b * nq + i, 0)),
        out_shape=jax.ShapeDtypeStruct((n_seq * seq_len, c), BF16),
        scratch_shapes=[pltpu.VMEM((n_heads, qb, HEAD_DIM), F32),
                        pltpu.VMEM((n_heads, qb, HEAD_DIM), F32),
                        pltpu.VMEM((n_heads, K_BLOCK, HEAD_DIM), k_past.dtype),
                        pltpu.VMEM((n_heads, K_BLOCK, HEAD_DIM), v_past.dtype),
                        pltpu.SemaphoreType.DMA((2,))],
        compiler_params=_params(2),
        name="stick_breaking",
    )(q, k_new, v_new, k_past, v_past, k_past, v_past, otd, otp, k_past, v_past)


def _trunk(x, conv_state, k_past, v_past, mem_k, mem_v, w, *, attn_rows):
    n_seq, seq_len, d = x.shape
    depth, n_a, d_main, alpha = w["depth"], w["n_a"], w["d_main"], w["alpha"]
    xf = x.reshape(n_seq * seq_len, d)
    xb = xf
    conv_states = []
    k_new = v_new = None
    for layer in range(depth):
        if layer < n_a:
            main, st, q_mem = _inproj_a(xb, conv_state, w["w_in_a"], w["conv_w_t"], layer,
                                        n_seq, seq_len)
            conv_states.append(st)
        else:
            k, v, q, q_mem = _inproj_b(xb, w["w_kv"], w["w_in_b"], layer - n_a, d_main,
                                       n_seq, seq_len)
            if k_new is None:
                k_new, v_new = k, v
            kp = None if k_past is None else jnp.transpose(k_past, (0, 2, 1, 3))
            vp = None if v_past is None else jnp.transpose(v_past, (0, 2, 1, 3))
            main = _stick_breaking(q, k_new, v_new, kp, vp)
        mo = _mem_attn(q_mem, mem_k, mem_v, layer, n_seq, seq_len, rows=attn_rows)
        xf, xb = _out_proj(xf, main, mo, w["w_out"], w["ln_g"], w["ln_b"], layer, alpha)
        xf, xb = _mlp(xb, xf, w["w_ff1"], w["w_ff2"], w["ln_g"], w["ln_b"], layer, alpha,
                      want_bf16=layer + 1 < depth)
    return (xf.reshape(n_seq, seq_len, d), jnp.stack(conv_states),
            jnp.transpose(k_new, (0, 2, 1, 3)), jnp.transpose(v_new, (0, 2, 1, 3)))


def kernel(x_prompt, x_sample, state_conv, cache_k, cache_v, cache_mem_k, cache_mem_v, mem_prompt,
           w_in_a, conv_w, w_in_b, w_kv, w_mem_kv, w_out, w_ff1, w_ff2, ln_g, ln_b):
    depth, d_mix, d = w_out.shape
    n_a, d_main = conv_w.shape[0], conv_w.shape[1]
    d_mem = d_mix - d_main
    n_mem_heads = d_mem // HEAD_DIM
    n_prompt, n_mem = mem_prompt.shape[0], mem_prompt.shape[1]
    w = dict(
        depth=depth, n_a=n_a, d_main=d_main, alpha=(2.0 * depth) ** 0.25,
        w_in_a=w_in_a.astype(BF16), conv_w_t=jnp.swapaxes(conv_w, 1, 2),
        w_in_b=w_in_b.astype(BF16), w_kv=w_kv.astype(BF16), w_out=w_out.astype(BF16),
        w_ff1=w_ff1.astype(BF16), w_ff2=w_ff2.astype(BF16),
        ln_g=ln_g.reshape(depth * 2, 1, d), ln_b=ln_b.reshape(depth * 2, 1, d))

    mem_k_p, mem_v_p = _memkv(mem_prompt.reshape(n_prompt * n_mem, d), w_mem_kv)
    rows_view = lambda a, b: a.reshape(depth, b, n_mem * n_mem_heads, HEAD_DIM)
    mem_k_p, mem_v_p = rows_view(mem_k_p, n_prompt), rows_view(mem_v_p, n_prompt)
    y_p, conv_p, k_p, v_p = _trunk(x_prompt, None, None, None, mem_k_p, mem_v_p, w, attn_rows=1024)
    n_sample = x_sample.shape[0]
    y_s, conv_s, k_s, v_s = _trunk(x_sample, state_conv, cache_k, cache_v,
                                   rows_view(cache_mem_k, n_sample), rows_view(cache_mem_v, n_sample),
                                   w, attn_rows=512)
    mem_shape = (depth, n_prompt, n_mem, n_mem_heads, HEAD_DIM)
    return (y_p, y_s, conv_p, conv_s, k_p, v_p, k_s, v_s,
            mem_k_p.reshape(mem_shape), mem_v_p.reshape(mem_shape))
```

```python
import functools
import math

import jax
import jax.numpy as jnp
import numpy as np
from jax import lax
from jax.experimental import pallas as pl
from jax.experimental.pallas import tpu as pltpu

F32 = jnp.float32
BF16 = jnp.bfloat16

HEAD_DIM = 128
LN_EPS = 1e-5
K_BLOCK = 128
SB_TAIL_ROWS = 32
HALO_ROWS = 16
F32_EXP_UNDERFLOW = 104.0
WEIGHT_TILE = 512
EPILOGUE_ROWS = 128
CONV_CHUNK_ROWS = 256
MLP_EPILOGUE_ROWS = 256
VMEM_LIMIT_BYTES = 48 * 1024 * 1024
RIDE_VMEM_LIMIT_BYTES = 56 * 1024 * 1024
MLP_VMEM_LIMIT_BYTES = 58 * 1024 * 1024

_NT = (((1,), (1,)), ((), ()))


def _dot(a, b):
    return jnp.dot(a, b, preferred_element_type=F32)


def _params(grid_rank, vmem_limit_bytes=VMEM_LIMIT_BYTES):
    return pltpu.CompilerParams(dimension_semantics=("arbitrary",) * grid_rank,
                                vmem_limit_bytes=vmem_limit_bytes)


def _layer_norm(y, g, b):
    mu = jnp.mean(y, axis=-1, keepdims=True)
    d = y - mu
    var = jnp.mean(d * d, axis=-1, keepdims=True)
    return d * lax.rsqrt(var + LN_EPS) * g + b


def _seq_tiling(n_seq, seq_len, rows):
    if seq_len >= rows:
        assert seq_len % rows == 0
        return 1, rows, seq_len // rows
    assert rows % seq_len == 0 and n_seq % (rows // seq_len) == 0
    return rows // seq_len, seq_len, 1


def _memkv_kernel(mem_ref, w_ref, k_ref, v_ref, *, n_heads):
    rows = mem_ref.shape[0]
    r = _dot(mem_ref[...].astype(BF16), w_ref[...].astype(BF16))
    for i, ref in enumerate((k_ref, v_ref)):
        for h in range(n_heads):
            col = (i * n_heads + h) * HEAD_DIM
            ref[pl.ds(h, rows, stride=n_heads), :] = r[:, col:col + HEAD_DIM]


def _memkv(mem, w_mem_kv):
    rows, d = mem.shape
    depth, _, two_dmem = w_mem_kv.shape
    n_heads = two_dmem // 2 // HEAD_DIM
    out = jax.ShapeDtypeStruct((depth, rows * n_heads, HEAD_DIM), F32)
    return pl.pallas_call(
        functools.partial(_memkv_kernel, n_heads=n_heads),
        grid=(depth,),
        in_specs=[pl.BlockSpec((rows, d), lambda l: (0, 0)),
                  pl.BlockSpec((None, d, two_dmem), lambda l: (l, 0, 0))],
        out_specs=[pl.BlockSpec((None, rows * n_heads, HEAD_DIM), lambda l: (l, 0, 0))] * 2,
        out_shape=[out, out],
        compiler_params=_params(1),
        name="memkv",
    )(mem, w_mem_kv)


def _ride_along_specs(w_ff1, w_ff2, layer, grid):
    n_steps = grid[0] * grid[1]
    _, d, d_ff = w_ff1.shape
    if d_ff % (n_steps * 128):
        return None
    slab = d_ff // n_steps
    step = lambda i, j: i * grid[1] + j
    in_specs = [pl.BlockSpec((None, d, slab), lambda i, j: (layer, 0, step(i, j))),
                pl.BlockSpec((None, slab, d), lambda i, j: (layer, step(i, j), 0))]
    out_specs = [pl.BlockSpec((d, slab), lambda i, j: (0, step(i, j))),
                 pl.BlockSpec((slab, d), lambda i, j: (step(i, j), 0))]
    out_shape = [jax.ShapeDtypeStruct((d, d_ff), BF16), jax.ShapeDtypeStruct((d_ff, d), BF16)]
    return in_specs, out_specs, out_shape


def _ride_along_cast(rest, n_out, ride):
    if not ride:
        return rest
    rest[2 + n_out][...] = rest[0][...].astype(BF16)
    rest[3 + n_out][...] = rest[1][...].astype(BF16)
    return rest[2:2 + n_out] + rest[4 + n_out:]


def _inproj_a_kernel(x_ref, halo_ref, wx_ref, wb_ref, wc_ref, cw_ref, wqm_ref, *rest,
                     S, L, tiles_per_seq, has_state, ride):
    main_ref, st_ref, qm_ref, u_scr = _ride_along_cast(rest, 3, ride)
    j = pl.program_id(1)

    @pl.when(j > 0)
    def _():
        wx = wx_ref[...]
        wb = wb_ref[...]
        wc = wc_ref[...]
        cw = cw_ref[...]
        w0, w1, w2 = cw[0:1], cw[1:2], cw[2:3]
        if has_state:
            for s in range(S):
                u_scr[s, HALO_ROWS - 2:HALO_ROWS, :] = halo_ref[s]
        else:
            xh = halo_ref[...].astype(BF16)
            uh = _dot(xh, wc) * _dot(xh, wx)
            first = (pl.program_id(0) % tiles_per_seq) == 0
            u_scr[0, 0:HALO_ROWS, :] = jnp.where(first, 0.0, uh)
        piece = min(CONV_CHUNK_ROWS, L)
        assert CONV_CHUNK_ROWS % piece == 0 and L % piece == 0 and (S * L) % CONV_CHUNK_ROWS == 0
        for r in range(0, S * L, CONV_CHUNK_ROWS):
            x = x_ref[r:r + CONV_CHUNK_ROWS, :].astype(BF16)
            u = _dot(x, wc) * _dot(x, wx)
            gate_b = _dot(x, wb)
            for p in range(0, CONV_CHUNK_ROWS, piece):
                s, t0 = divmod(r + p, L)
                lo = HALO_ROWS + t0
                u_scr[s, lo:lo + piece, :] = u[p:p + piece]
                y = u_scr[s, lo - 2:lo - 2 + piece, :] * w0
                y = y + u_scr[s, lo - 1:lo - 1 + piece, :] * w1
                y = y + u[p:p + piece] * w2
                main_ref[r + p:r + p + piece, :] = (gate_b[p:p + piece] * y).astype(main_ref.dtype)
                if t0 + piece == L:
                    st_ref[s] = u_scr[s, HALO_ROWS + L - 2:HALO_ROWS + L, :]

    @pl.when(j == 0)
    def _():
        qm_ref[...] = _dot(x_ref[...].astype(BF16), wqm_ref[...]).astype(BF16)


def _inproj_a(x, state, w_in_a_t, conv_w_t, layer, n_seq, seq_len, ff_weights, *, rows=1024):
    m, d = x.shape
    c = conv_w_t.shape[-1]
    tn = WEIGHT_TILE
    S, L, tps = _seq_tiling(n_seq, seq_len, rows)
    nct = c // tn
    has_state = state is not None
    cj = lambda j: jnp.maximum(j - 1, 0)
    if has_state:
        assert tps == 1
        halo = state
        halo_spec = pl.BlockSpec((None, S, 2, tn), lambda i, j: (layer, i, 0, cj(j)))
    else:
        halo = x
        halo_spec = pl.BlockSpec(
            (HALO_ROWS, d), lambda i, j: (jnp.maximum(i * (rows // HALO_ROWS) - 1, 0), 0))
    grid = (m // rows, nct + 1)
    ride = ff_weights and _ride_along_specs(*ff_weights, layer, grid)
    ride_in, ride_out, ride_shape = ride or ([], [], [])
    kern = functools.partial(_inproj_a_kernel, S=S, L=L, tiles_per_seq=tps, has_state=has_state,
                             ride=bool(ride))
    w_spec = lambda off: pl.BlockSpec((None, d, tn), lambda i, j: (layer, 0, off + cj(j)))
    main, tile_state, q_mem, *cast = pl.pallas_call(
        kern,
        grid=grid,
        in_specs=[pl.BlockSpec((rows, d), lambda i, j: (i, 0)),
                  halo_spec,
                  w_spec(0), w_spec(nct), w_spec(2 * nct),
                  pl.BlockSpec((None, 3, tn), lambda i, j: (layer, 0, cj(j))),
                  pl.BlockSpec((None, d, tn), lambda i, j: (layer, 0, 3 * nct))] + ride_in,
        out_specs=[pl.BlockSpec((rows, tn), lambda i, j: (i, cj(j))),
                   pl.BlockSpec((S, 2, tn), lambda i, j: (i, 0, cj(j))),
                   pl.BlockSpec((rows, tn), lambda i, j: (i, 0))] + ride_out,
        out_shape=[jax.ShapeDtypeStruct((m, c), BF16),
                   jax.ShapeDtypeStruct((n_seq * tps, 2, c), F32),
                   jax.ShapeDtypeStruct((m, tn), BF16)] + ride_shape,
        scratch_shapes=[pltpu.VMEM((S, HALO_ROWS + L, tn), F32)],
        compiler_params=_params(2, RIDE_VMEM_LIMIT_BYTES if ride else VMEM_LIMIT_BYTES),
        name="inproj_a",
    )(x, halo, w_in_a_t, w_in_a_t, w_in_a_t, conv_w_t, w_in_a_t, *(ff_weights if ride else ()))
    return main, tile_state.reshape(n_seq, tps, 2, c)[:, tps - 1], q_mem, tuple(cast) or None


def _mem_attn_kernel(q_ref, mk_ref, mv_ref, o_ref, *, S, L):
    n_heads = q_ref.shape[1] // HEAD_DIM
    n_mem = mk_ref.shape[1] // n_heads
    scale = 1.0 / math.sqrt(HEAD_DIM)
    tiles = [(slice(s * L, (s + 1) * L), slice(h * HEAD_DIM, (h + 1) * HEAD_DIM),
              s, pl.ds(h, n_mem, stride=n_heads))
             for s in range(S) for h in range(n_heads)]
    sc = jnp.concatenate(
        [lax.dot_general(q_ref[rows, cols], mk_ref[s, head_rows, :].astype(BF16), _NT,
                         preferred_element_type=F32) for rows, cols, s, head_rows in tiles],
        axis=0) * scale
    e = jnp.exp(sc - jnp.max(sc, axis=-1, keepdims=True))
    p = (e / jnp.sum(e, axis=-1, keepdims=True)).astype(BF16)
    for i, (rows, cols, s, head_rows) in enumerate(tiles):
        v = mv_ref[s, head_rows, :].astype(BF16)
        o_ref[rows, cols] = _dot(p[i * L:(i + 1) * L], v).astype(BF16)


def _mem_attn(q_mem, mem_k, mem_v, layer, n_seq, seq_len, *, rows):
    m, d_mem = q_mem.shape
    S, L, tps = _seq_tiling(n_seq, seq_len, rows)
    mem_spec = pl.BlockSpec((None, S) + mem_k.shape[2:], lambda i: (layer, i // tps, 0, 0))
    row_spec = pl.BlockSpec((rows, d_mem), lambda i: (i, 0))
    return pl.pallas_call(
        functools.partial(_mem_attn_kernel, S=S, L=L),
        grid=(m // rows,),
        in_specs=[row_spec, mem_spec, mem_spec],
        out_specs=row_spec,
        out_shape=jax.ShapeDtypeStruct((m, d_mem), BF16),
        compiler_params=_params(1),
        name="mem_attn",
    )(q_mem, mem_k, mem_v)


def _out_proj_kernel(x_ref, main_ref, mo_ref, wo_ref, g_ref, b_ref, yf_ref, yb_ref, *, alpha):
    d_main = main_ref.shape[1]
    for r in range(0, x_ref.shape[0], EPILOGUE_ROWS):
        rows = slice(r, r + EPILOGUE_ROWS)
        mix = (_dot(main_ref[rows, :], wo_ref[0:d_main, :])
               + _dot(mo_ref[rows, :], wo_ref[d_main:, :]))
        y = _layer_norm(alpha * x_ref[rows, :] + mix, g_ref[...], b_ref[...])
        yf_ref[rows, :] = y
        yb_ref[rows, :] = y.astype(BF16)


def _out_proj(xf, main, mo, w_out_b, ln_g, ln_b, layer, alpha, *, rows=512):
    m, d = xf.shape
    d_main, d_mem = main.shape[1], mo.shape[1]
    ln_spec = pl.BlockSpec((None, 1, d), lambda i: (2 * layer, 0, 0))
    return pl.pallas_call(
        functools.partial(_out_proj_kernel, alpha=alpha),
        grid=(m // rows,),
        in_specs=[pl.BlockSpec((rows, d), lambda i: (i, 0)),
                  pl.BlockSpec((rows, d_main), lambda i: (i, 0)),
                  pl.BlockSpec((rows, d_mem), lambda i: (i, 0)),
                  pl.BlockSpec((None, d_main + d_mem, d), lambda i: (layer, 0, 0),
                               pipeline_mode=pl.Buffered(1)),
                  ln_spec, ln_spec],
        out_specs=[pl.BlockSpec((rows, d), lambda i: (i, 0))] * 2,
        out_shape=[jax.ShapeDtypeStruct((m, d), F32), jax.ShapeDtypeStruct((m, d), BF16)],
        compiler_params=_params(1),
        name="out_proj",
    )(xf, main, mo, w_out_b, ln_g, ln_b)


def _mlp_kernel(xb_ref, xf_ref, w1_ref, w2_ref, g_ref, b_ref, yf_ref, *yb_ref, alpha):
    f = pl.program_id(1)
    last = pl.num_programs(1) - 1

    def ff(rows):
        h = jnp.square(jnp.maximum(_dot(xb_ref[rows, :], w1_ref[...]), 0.0)).astype(BF16)
        return _dot(h, w2_ref[...])

    @pl.when(f == 0)
    def _():
        yf_ref[...] = ff(slice(None))

    @pl.when((f > 0) & (f < last))
    def _():
        yf_ref[...] += ff(slice(None))

    @pl.when(f == last)
    def _():
        for r in range(0, yf_ref.shape[0], MLP_EPILOGUE_ROWS):
            rows = slice(r, r + MLP_EPILOGUE_ROWS)
            acc = yf_ref[rows, :] + ff(rows)
            y = _layer_norm(alpha * xf_ref[rows, :] + acc, g_ref[...], b_ref[...])
            yf_ref[rows, :] = y
            for ref in yb_ref:
                ref[rows, :] = y.astype(BF16)


def _mlp(xb, xf, w_ff1_b, w_ff2_b, ln_g, ln_b, layer, alpha, *, want_bf16, rows=1024):
    m, d = xf.shape
    tf = WEIGHT_TILE
    n_f = w_ff1_b.shape[1] // tf
    ln_spec = pl.BlockSpec((None, 1, d), lambda i, f: (2 * layer + 1, 0, 0))
    row_spec = pl.BlockSpec((rows, d), lambda i, f: (i, 0))
    out_shape = [jax.ShapeDtypeStruct((m, d), F32)] + [jax.ShapeDtypeStruct((m, d), BF16)] * want_bf16
    out = pl.pallas_call(
        functools.partial(_mlp_kernel, alpha=alpha),
        grid=(m // rows, n_f),
        in_specs=[row_spec,
                  pl.BlockSpec((rows, d), lambda i, f: (i, 0), pipeline_mode=pl.Buffered(1)),
                  pl.BlockSpec((d, tf), lambda i, f: (0, f)),
                  pl.BlockSpec((tf, d), lambda i, f: (f, 0)),
                  ln_spec, ln_spec],
        out_specs=[row_spec] * len(out_shape),
        out_shape=out_shape,
        compiler_params=_params(2, MLP_VMEM_LIMIT_BYTES),
        name="mlp",
    )(xb, xf, w_ff1_b, w_ff2_b, ln_g, ln_b)
    return out if want_bf16 else (out[0], None)


def _inproj_b_kernel(x_ref, wk_ref, wv_ref, wq_ref, *rest, S, L, ride):
    k_ref, v_ref, q_ref, qm_ref = _ride_along_cast(rest, 4, ride)
    j = pl.program_id(1)

    @pl.when(j > 0)
    def _():
        x = x_ref[...]
        for w_ref, o_ref in ((wk_ref, k_ref), (wv_ref, v_ref), (wq_ref, q_ref)):
            r = _dot(x, w_ref[...])
            for s in range(S):
                for h in range(o_ref.shape[1]):
                    o_ref[s, h] = r[s * L:(s + 1) * L,
                                    h * HEAD_DIM:(h + 1) * HEAD_DIM].astype(o_ref.dtype)

    @pl.when(j == 0)
    def _():
        qm_ref[...] = _dot(x_ref[...], wq_ref[...]).astype(BF16)


def _inproj_b(xb, w_kv_t, w_in_b_t, b_layer, d_main, n_seq, seq_len, ff_weights, ff_layer, *,
              rows=1024):
    m, d = xb.shape
    n_heads = d_main // HEAD_DIM
    tn = WEIGHT_TILE
    nct = d_main // tn
    assert w_in_b_t.shape[-1] == (nct + 1) * tn
    S, L, tps = _seq_tiling(n_seq, seq_len, rows)
    cj = lambda j: jnp.maximum(j - 1, 0)
    out_spec = pl.BlockSpec((S, tn // HEAD_DIM, L, HEAD_DIM),
                            lambda i, j: (i // tps, cj(j), i % tps, 0))
    shape = (n_seq, n_heads, seq_len, HEAD_DIM)
    grid = (m // rows, nct + 1)
    ride = ff_weights and _ride_along_specs(*ff_weights, ff_layer, grid)
    ride_in, ride_out, ride_shape = ride or ([], [], [])
    k, v, q, q_mem, *cast = pl.pallas_call(
        functools.partial(_inproj_b_kernel, S=S, L=L, ride=bool(ride)),
        grid=grid,
        in_specs=[pl.BlockSpec((rows, d), lambda i, j: (i, 0)),
                  pl.BlockSpec((d, tn), lambda i, j: (0, cj(j))),
                  pl.BlockSpec((d, tn), lambda i, j: (0, nct + cj(j))),
                  pl.BlockSpec((None, d, tn),
                               lambda i, j: (b_layer, 0, jnp.where(j == 0, nct, j - 1)))] + ride_in,
        out_specs=[out_spec] * 3 + [pl.BlockSpec((rows, tn), lambda i, j: (i, 0))] + ride_out,
        out_shape=[jax.ShapeDtypeStruct(shape, F32), jax.ShapeDtypeStruct(shape, F32),
                   jax.ShapeDtypeStruct(shape, BF16),
                   jax.ShapeDtypeStruct((m, tn), BF16)] + ride_shape,
        compiler_params=_params(2, RIDE_VMEM_LIMIT_BYTES if ride else VMEM_LIMIT_BYTES),
        name="inproj_b",
    )(xb, w_kv_t, w_kv_t, w_in_b_t, *(ff_weights if ride else ()))
    return k, v, q, q_mem, tuple(cast) or None


def _softplus(z):
    return jnp.maximum(z, 0.0) + jnp.log(1.0 + jnp.exp(-jnp.abs(z)))


def _sb_block(q_ref, k_ref, v_ref, ones_tri, carried_scr, out_scr, *, newest, rows=None):
    n_heads, qb, _ = q_ref.shape
    rows = qb if rows is None else rows
    kb = k_ref.shape[1]
    scale = 1.0 / math.sqrt(HEAD_DIM)
    stack = lambda per_head: jnp.concatenate([per_head(h) for h in range(n_heads)], axis=0)
    z = stack(lambda h: lax.dot_general(q_ref[h, :rows], k_ref[h].astype(BF16), _NT,
                                        preferred_element_type=F32)) * scale
    sp = _softplus(z)
    log_a = z - sp
    if newest:
        row = lax.broadcasted_iota(jnp.int32, z.shape, 0) & (qb - 1)
        col = lax.broadcasted_iota(jnp.int32, z.shape, 1)
        valid = col < row
        sp = jnp.where(valid, sp, 0.0)
    hi = sp.astype(BF16)
    lo = (sp - hi.astype(F32)).astype(BF16)
    if ones_tri.shape[0] == 2 * kb:
        sums = _dot(jnp.concatenate([hi, lo], axis=1), ones_tri)
    else:
        sums = _dot(hi, ones_tri) + _dot(lo, ones_tri)
    log_a = log_a - sums[:, HEAD_DIM:HEAD_DIM + kb]
    if not newest:
        log_a = log_a - stack(lambda h: carried_scr[h, :rows, :kb])
    a = jnp.exp(log_a)
    if newest:
        a = jnp.where(valid, a, 0.0)
    a = a.astype(BF16)
    for h in range(n_heads):
        head = slice(h * rows, (h + 1) * rows)
        contrib = _dot(a[head], v_ref[h].astype(BF16))
        if newest:
            out_scr[h] = contrib
            carried_scr[h] = sums[head, :HEAD_DIM]
        else:
            out_scr[h, :rows] += contrib
            carried_scr[h, :rows] += sums[head, :HEAD_DIM]


def _sb_kernel(q_ref, kd_ref, vd_ref, k1_ref, v1_ref, k2_ref, v2_ref, otd_ref, otp_ref,
               k_hbm, v_hbm, o_ref, carried_scr, out_scr, k_buf, v_buf, sem, *, n_past_static):
    b = pl.program_id(0)
    n_past = pl.program_id(1) if n_past_static is None else n_past_static
    qb = q_ref.shape[1]
    tail = min(SB_TAIL_ROWS, qb)

    def alive(first_row):
        return jnp.min(carried_scr[:, first_row:, :]) <= F32_EXP_UNDERFLOW

    def past_block(k_ref, v_ref, rows=None):
        _sb_block(q_ref, k_ref, v_ref, otp_ref[...], carried_scr, out_scr, newest=False, rows=rows)

    _sb_block(q_ref, kd_ref, vd_ref, otd_ref[...], carried_scr, out_scr, newest=True)

    @pl.when(n_past >= 1)
    def _():
        past_block(k1_ref, v1_ref)

    any_alive, tail_alive = alive(0), alive(tail)

    @pl.when((n_past >= 2) & tail_alive)
    def _():
        past_block(k2_ref, v2_ref)

    @pl.when((n_past >= 2) & any_alive & jnp.logical_not(tail_alive))
    def _():
        past_block(k2_ref, v2_ref, rows=tail)

    n_left = n_past - 2

    def fetch(hbm, buf, slot, blk):
        start = pl.multiple_of(blk * K_BLOCK, K_BLOCK)
        return pltpu.make_async_copy(hbm.at[b, :, pl.ds(start, K_BLOCK), :], buf, sem.at[slot])

    def body(carry):
        i, _ = carry
        blk = n_left - 1 - i
        fetch(k_hbm, k_buf, 0, blk).start()
        fetch(v_hbm, v_buf, 1, blk).start()
        fetch(k_hbm, k_buf, 0, blk).wait()
        fetch(v_hbm, v_buf, 1, blk).wait()
        past_block(k_buf, v_buf)
        return i + 1, alive(0)

    @pl.when(n_left > 0)
    def _():
        lax.while_loop(lambda c: (c[0] < n_left) & c[1], body, (jnp.int32(0), alive(0)))

    for h in range(out_scr.shape[0]):
        o_ref[:, h * HEAD_DIM:(h + 1) * HEAD_DIM] = out_scr[h].astype(o_ref.dtype)


def _ones_tri(kb):
    tri = np.tril(np.ones((kb, kb), np.float32), -1)
    m = np.concatenate([np.ones((kb, HEAD_DIM), np.float32), tri], axis=1)
    if kb % 128 == 0:
        m = np.concatenate([m, m], axis=0)
    return jnp.asarray(m, BF16)


def _stick_breaking(q, k_new, v_new, k_past, v_past):
    n_seq, n_heads, seq_len, _ = q.shape
    c = n_heads * HEAD_DIM
    qb = min(K_BLOCK, seq_len)
    assert qb & (qb - 1) == 0
    nq = seq_len // qb
    if k_past is None:
        assert qb == K_BLOCK
        k_past, v_past, past_blocks, n_past_static = k_new, v_new, nq, None
    else:
        assert nq == 1 and k_past.shape[2] % K_BLOCK == 0
        past_blocks = k_past.shape[2] // K_BLOCK
        n_past_static = past_blocks

    def past_spec(back):
        def index_map(b, i):
            n_past = i if n_past_static is None else n_past_static
            return (b, 0, jnp.clip(n_past - back, 0, past_blocks - 1), 0)
        return pl.BlockSpec((None, n_heads, K_BLOCK, HEAD_DIM), index_map)

    new_spec = pl.BlockSpec((None, n_heads, qb, HEAD_DIM), lambda b, i: (b, 0, i, 0))
    hbm_spec = pl.BlockSpec(memory_space=pl.ANY)
    otd, otp = _ones_tri(qb), _ones_tri(K_BLOCK)
    return pl.pallas_call(
        functools.partial(_sb_kernel, n_past_static=n_past_static),
        grid=(n_seq, nq),
        in_specs=[new_spec, new_spec, new_spec,
                  past_spec(1), past_spec(1), past_spec(2), past_spec(2),
                  pl.BlockSpec(otd.shape, lambda b, i: (0, 0)),
                  pl.BlockSpec(otp.shape, lambda b, i: (0, 0)),
                  hbm_spec, hbm_spec],
        out_specs=pl.BlockSpec((qb, c), lambda b, i: (b * nq + i, 0)),
        out_shape=jax.ShapeDtypeStruct((n_seq * seq_len, c), BF16),
        scratch_shapes=[pltpu.VMEM((n_heads, qb, HEAD_DIM), F32),
                        pltpu.VMEM((n_heads, qb, HEAD_DIM), F32),
                        pltpu.VMEM((n_heads, K_BLOCK, HEAD_DIM), k_past.dtype),
                        pltpu.VMEM((n_heads, K_BLOCK, HEAD_DIM), v_past.dtype),
                        pltpu.SemaphoreType.DMA((2,))],
        compiler_params=_params(2),
        name="stick_breaking",
    )(q, k_new, v_new, k_past, v_past, k_past, v_past, otd, otp, k_past, v_past)


def _trunk(x, conv_state, k_past, v_past, mem_k, mem_v, w, ff_bf16, *, attn_rows):
    n_seq, seq_len, d = x.shape
    depth, n_a, d_main, alpha = w["depth"], w["n_a"], w["d_main"], w["alpha"]
    xf = x.reshape(n_seq * seq_len, d)
    xb = xf
    conv_states = []
    k_new = v_new = None
    for layer in range(depth):
        ff_f32 = None if layer in ff_bf16 else (w["w_ff1"], w["w_ff2"])
        if layer < n_a:
            main, st, q_mem, cast = _inproj_a(xb, conv_state, w["w_in_a"], w["conv_w_t"], layer,
                                              n_seq, seq_len, ff_f32)
            conv_states.append(st)
        else:
            k, v, q, q_mem, cast = _inproj_b(xb, w["w_kv"], w["w_in_b"], layer - n_a, d_main,
                                             n_seq, seq_len, ff_f32, layer)
            if k_new is None:
                k_new, v_new = k, v
            kp = None if k_past is None else jnp.transpose(k_past, (0, 2, 1, 3))
            vp = None if v_past is None else jnp.transpose(v_past, (0, 2, 1, 3))
            main = _stick_breaking(q, k_new, v_new, kp, vp)
        mo = _mem_attn(q_mem, mem_k, mem_v, layer, n_seq, seq_len, rows=attn_rows)
        xf, xb = _out_proj(xf, main, mo, w["w_out"], w["ln_g"], w["ln_b"], layer, alpha)
        if layer not in ff_bf16:
            ff_bf16[layer] = cast or (w["w_ff1"][layer].astype(BF16), w["w_ff2"][layer].astype(BF16))
        xf, xb = _mlp(xb, xf, *ff_bf16[layer], w["ln_g"], w["ln_b"], layer, alpha,
                      want_bf16=layer + 1 < depth)
    return (xf.reshape(n_seq, seq_len, d), jnp.stack(conv_states),
            jnp.transpose(k_new, (0, 2, 1, 3)), jnp.transpose(v_new, (0, 2, 1, 3)))


def kernel(x_prompt, x_sample, state_conv, cache_k, cache_v, cache_mem_k, cache_mem_v, mem_prompt,
           w_in_a, conv_w, w_in_b, w_kv, w_mem_kv, w_out, w_ff1, w_ff2, ln_g, ln_b):
    depth, d_mix, d = w_out.shape
    n_a, d_main = conv_w.shape[0], conv_w.shape[1]
    d_mem = d_mix - d_main
    n_mem_heads = d_mem // HEAD_DIM
    n_prompt, n_mem = mem_prompt.shape[0], mem_prompt.shape[1]
    w = dict(
        depth=depth, n_a=n_a, d_main=d_main, alpha=(2.0 * depth) ** 0.25,
        w_in_a=w_in_a.astype(BF16), conv_w_t=jnp.swapaxes(conv_w, 1, 2),
        w_in_b=w_in_b.astype(BF16), w_kv=w_kv.astype(BF16), w_out=w_out.astype(BF16),
        w_ff1=w_ff1, w_ff2=w_ff2,
        ln_g=ln_g.reshape(depth * 2, 1, d), ln_b=ln_b.reshape(depth * 2, 1, d))

    mem_k_p, mem_v_p = _memkv(mem_prompt.reshape(n_prompt * n_mem, d), w_mem_kv)
    rows_view = lambda a, b: a.reshape(depth, b, n_mem * n_mem_heads, HEAD_DIM)
    mem_k_p, mem_v_p = rows_view(mem_k_p, n_prompt), rows_view(mem_v_p, n_prompt)
    ff_bf16 = {}
    y_p, conv_p, k_p, v_p = _trunk(x_prompt, None, None, None, mem_k_p, mem_v_p, w, ff_bf16,
                                   attn_rows=1024)
    n_sample = x_sample.shape[0]
    y_s, conv_s, k_s, v_s = _trunk(x_sample, state_conv, cache_k, cache_v,
                                   rows_view(cache_mem_k, n_sample), rows_view(cache_mem_v, n_sample),
                                   w, ff_bf16, attn_rows=512)
    mem_shape = (depth, n_prompt, n_mem, n_mem_heads, HEAD_DIM)
    return (y_p, y_s, conv_p, conv_s, k_p, v_p, k_s, v_s,
            mem_k_p.reshape(mem_shape), mem_v_p.reshape(mem_shape))
```

```python
import functools
import math

import jax
import jax.numpy as jnp
import numpy as np
from jax import lax
from jax.experimental import pallas as pl
from jax.experimental.pallas import tpu as pltpu

F32 = jnp.float32
BF16 = jnp.bfloat16

HEAD_DIM = 128
LN_EPS = 1e-5
K_BLOCK = 128
SB_TAIL_ROWS = 32
HALO_ROWS = 16
F32_EXP_UNDERFLOW = 104.0
WEIGHT_TILE = 512
EPILOGUE_ROWS = 128
CONV_CHUNK_ROWS = 256
MLP_EPILOGUE_ROWS = 256
VMEM_LIMIT_BYTES = 48 * 1024 * 1024
RIDE_VMEM_LIMIT_BYTES = 56 * 1024 * 1024
MLP_VMEM_LIMIT_BYTES = 58 * 1024 * 1024

_NT = (((1,), (1,)), ((), ()))


def _dot(a, b):
    return jnp.dot(a, b, preferred_element_type=F32)


def _params(grid_rank, vmem_limit_bytes=VMEM_LIMIT_BYTES):
    return pltpu.CompilerParams(dimension_semantics=("arbitrary",) * grid_rank,
                                vmem_limit_bytes=vmem_limit_bytes)


def _layer_norm(y, g, b):
    mu = jnp.mean(y, axis=-1, keepdims=True)
    d = y - mu
    var = jnp.mean(d * d, axis=-1, keepdims=True)
    return d * lax.rsqrt(var + LN_EPS) * g + b


def _seq_tiling(n_seq, seq_len, rows):
    if seq_len >= rows:
        assert seq_len % rows == 0
        return 1, rows, seq_len // rows
    assert rows % seq_len == 0 and n_seq % (rows // seq_len) == 0
    return rows // seq_len, seq_len, 1


def _memkv_kernel(mem_ref, w_ref, k_ref, v_ref, *, n_heads):
    rows = mem_ref.shape[0]
    r = _dot(mem_ref[...].astype(BF16), w_ref[...].astype(BF16))
    for i, ref in enumerate((k_ref, v_ref)):
        for h in range(n_heads):
            col = (i * n_heads + h) * HEAD_DIM
            ref[pl.ds(h, rows, stride=n_heads), :] = r[:, col:col + HEAD_DIM]


def _memkv(mem, w_mem_kv):
    rows, d = mem.shape
    depth, _, two_dmem = w_mem_kv.shape
    n_heads = two_dmem // 2 // HEAD_DIM
    out = jax.ShapeDtypeStruct((depth, rows * n_heads, HEAD_DIM), F32)
    return pl.pallas_call(
        functools.partial(_memkv_kernel, n_heads=n_heads),
        grid=(depth,),
        in_specs=[pl.BlockSpec((rows, d), lambda l: (0, 0)),
                  pl.BlockSpec((None, d, two_dmem), lambda l: (l, 0, 0))],
        out_specs=[pl.BlockSpec((None, rows * n_heads, HEAD_DIM), lambda l: (l, 0, 0))] * 2,
        out_shape=[out, out],
        compiler_params=_params(1),
        name="memkv",
    )(mem, w_mem_kv)


def _ride_along_specs(w_ff1, w_ff2, layer, grid):
    n_steps = grid[0] * grid[1]
    _, d, d_ff = w_ff1.shape
    if d_ff % (n_steps * 128):
        return None
    slab = d_ff // n_steps
    step = lambda i, j: i * grid[1] + j
    in_specs = [pl.BlockSpec((None, d, slab), lambda i, j: (layer, 0, step(i, j))),
                pl.BlockSpec((None, slab, d), lambda i, j: (layer, step(i, j), 0))]
    out_specs = [pl.BlockSpec((d, slab), lambda i, j: (0, step(i, j))),
                 pl.BlockSpec((slab, d), lambda i, j: (step(i, j), 0))]
    out_shape = [jax.ShapeDtypeStruct((d, d_ff), BF16), jax.ShapeDtypeStruct((d_ff, d), BF16)]
    return in_specs, out_specs, out_shape


def _ride_along_cast(rest, n_out, ride):
    if not ride:
        return rest, lambda: None

    def cast():
        rest[2 + n_out][...] = rest[0][...].astype(BF16)
        rest[3 + n_out][...] = rest[1][...].astype(BF16)

    return rest[2:2 + n_out] + rest[4 + n_out:], cast


def _inproj_a_kernel(x_ref, halo_ref, wx_ref, wb_ref, wc_ref, cw_ref, wqm_ref, *rest,
                     S, L, tiles_per_seq, has_state, ride):
    (main_ref, st_ref, qm_ref, u_scr), cast_slab = _ride_along_cast(rest, 3, ride)
    j = pl.program_id(1)

    @pl.when(j > 0)
    def _():
        cast_slab()
        wx = wx_ref[...]
        wb = wb_ref[...]
        wc = wc_ref[...]
        cw = cw_ref[...]
        w0, w1, w2 = cw[0:1], cw[1:2], cw[2:3]
        if has_state:
            for s in range(S):
                u_scr[s, HALO_ROWS - 2:HALO_ROWS, :] = halo_ref[s]
        else:
            xh = halo_ref[...].astype(BF16)
            uh = _dot(xh, wc) * _dot(xh, wx)
            first = (pl.program_id(0) % tiles_per_seq) == 0
            u_scr[0, 0:HALO_ROWS, :] = jnp.where(first, 0.0, uh)
        piece = min(CONV_CHUNK_ROWS, L)
        assert CONV_CHUNK_ROWS % piece == 0 and L % piece == 0 and (S * L) % CONV_CHUNK_ROWS == 0
        for r in range(0, S * L, CONV_CHUNK_ROWS):
            x = x_ref[r:r + CONV_CHUNK_ROWS, :].astype(BF16)
            u = _dot(x, wc) * _dot(x, wx)
            gate_b = _dot(x, wb)
            for p in range(0, CONV_CHUNK_ROWS, piece):
                s, t0 = divmod(r + p, L)
                lo = HALO_ROWS + t0
                u_scr[s, lo:lo + piece, :] = u[p:p + piece]
                y = u_scr[s, lo - 2:lo - 2 + piece, :] * w0
                y = y + u_scr[s, lo - 1:lo - 1 + piece, :] * w1
                y = y + u[p:p + piece] * w2
                main_ref[r + p:r + p + piece, :] = (gate_b[p:p + piece] * y).astype(main_ref.dtype)
                if t0 + piece == L:
                    st_ref[s] = u_scr[s, HALO_ROWS + L - 2:HALO_ROWS + L, :]

    @pl.when(j == 0)
    def _():
        cast_slab()
        qm_ref[...] = _dot(x_ref[...].astype(BF16), wqm_ref[...]).astype(BF16)


def _inproj_a(x, state, w_in_a_t, conv_w_t, layer, n_seq, seq_len, ff_weights, *, rows=1024):
    m, d = x.shape
    c = conv_w_t.shape[-1]
    tn = WEIGHT_TILE
    S, L, tps = _seq_tiling(n_seq, seq_len, rows)
    nct = c // tn
    has_state = state is not None
    cj = lambda j: jnp.maximum(j - 1, 0)
    if has_state:
        assert tps == 1
        halo = state
        halo_spec = pl.BlockSpec((None, S, 2, tn), lambda i, j: (layer, i, 0, cj(j)))
    else:
        halo = x
        halo_spec = pl.BlockSpec(
            (HALO_ROWS, d), lambda i, j: (jnp.maximum(i * (rows // HALO_ROWS) - 1, 0), 0))
    grid = (m // rows, nct + 1)
    ride = ff_weights and _ride_along_specs(*ff_weights, layer, grid)
    ride_in, ride_out, ride_shape = ride or ([], [], [])
    kern = functools.partial(_inproj_a_kernel, S=S, L=L, tiles_per_seq=tps, has_state=has_state,
                             ride=bool(ride))
    w_spec = lambda off: pl.BlockSpec((None, d, tn), lambda i, j: (layer, 0, off + cj(j)))
    main, tile_state, q_mem, *cast = pl.pallas_call(
        kern,
        grid=grid,
        in_specs=[pl.BlockSpec((rows, d), lambda i, j: (i, 0)),
                  halo_spec,
                  w_spec(0), w_spec(nct), w_spec(2 * nct),
                  pl.BlockSpec((None, 3, tn), lambda i, j: (layer, 0, cj(j))),
                  pl.BlockSpec((None, d, tn), lambda i, j: (layer, 0, 3 * nct))] + ride_in,
        out_specs=[pl.BlockSpec((rows, tn), lambda i, j: (i, cj(j))),
                   pl.BlockSpec((S, 2, tn), lambda i, j: (i, 0, cj(j))),
                   pl.BlockSpec((rows, tn), lambda i, j: (i, 0))] + ride_out,
        out_shape=[jax.ShapeDtypeStruct((m, c), BF16),
                   jax.ShapeDtypeStruct((n_seq * tps, 2, c), F32),
                   jax.ShapeDtypeStruct((m, tn), BF16)] + ride_shape,
        scratch_shapes=[pltpu.VMEM((S, HALO_ROWS + L, tn), F32)],
        compiler_params=_params(2, RIDE_VMEM_LIMIT_BYTES if ride else VMEM_LIMIT_BYTES),
        name="inproj_a",
    )(x, halo, w_in_a_t, w_in_a_t, w_in_a_t, conv_w_t, w_in_a_t, *(ff_weights if ride else ()))
    return main, tile_state.reshape(n_seq, tps, 2, c)[:, tps - 1], q_mem, tuple(cast) or None


def _mem_attn_kernel(q_ref, mk_ref, mv_ref, o_ref, *, S, L):
    n_heads = q_ref.shape[1] // HEAD_DIM
    n_mem = mk_ref.shape[1] // n_heads
    scale = 1.0 / math.sqrt(HEAD_DIM)
    tiles = [(slice(s * L, (s + 1) * L), slice(h * HEAD_DIM, (h + 1) * HEAD_DIM),
              s, pl.ds(h, n_mem, stride=n_heads))
             for s in range(S) for h in range(n_heads)]
    sc = jnp.concatenate(
        [lax.dot_general(q_ref[rows, cols], mk_ref[s, head_rows, :].astype(BF16), _NT,
                         preferred_element_type=F32) for rows, cols, s, head_rows in tiles],
        axis=0) * scale
    e = jnp.exp(sc - jnp.max(sc, axis=-1, keepdims=True))
    p = (e / jnp.sum(e, axis=-1, keepdims=True)).astype(BF16)
    for i, (rows, cols, s, head_rows) in enumerate(tiles):
        v = mv_ref[s, head_rows, :].astype(BF16)
        o_ref[rows, cols] = _dot(p[i * L:(i + 1) * L], v).astype(BF16)


def _mem_attn(q_mem, mem_k, mem_v, layer, n_seq, seq_len, *, rows):
    m, d_mem = q_mem.shape
    S, L, tps = _seq_tiling(n_seq, seq_len, rows)
    mem_spec = pl.BlockSpec((None, S) + mem_k.shape[2:], lambda i: (layer, i // tps, 0, 0))
    row_spec = pl.BlockSpec((rows, d_mem), lambda i: (i, 0))
    return pl.pallas_call(
        functools.partial(_mem_attn_kernel, S=S, L=L),
        grid=(m // rows,),
        in_specs=[row_spec, mem_spec, mem_spec],
        out_specs=row_spec,
        out_shape=jax.ShapeDtypeStruct((m, d_mem), BF16),
        compiler_params=_params(1),
        name="mem_attn",
    )(q_mem, mem_k, mem_v)


def _out_proj_kernel(x_ref, main_ref, mo_ref, wo_ref, g_ref, b_ref, yf_ref, yb_ref, *, alpha):
    d_main = main_ref.shape[1]
    for r in range(0, x_ref.shape[0], EPILOGUE_ROWS):
        rows = slice(r, r + EPILOGUE_ROWS)
        mix = (_dot(main_ref[rows, :], wo_ref[0:d_main, :])
               + _dot(mo_ref[rows, :], wo_ref[d_main:, :]))
        y = _layer_norm(alpha * x_ref[rows, :] + mix, g_ref[...], b_ref[...])
        yf_ref[rows, :] = y
        yb_ref[rows, :] = y.astype(BF16)


def _out_proj(xf, main, mo, w_out_b, ln_g, ln_b, layer, alpha, *, rows=512):
    m, d = xf.shape
    d_main, d_mem = main.shape[1], mo.shape[1]
    ln_spec = pl.BlockSpec((None, 1, d), lambda i: (2 * layer, 0, 0))
    return pl.pallas_call(
        functools.partial(_out_proj_kernel, alpha=alpha),
        grid=(m // rows,),
        in_specs=[pl.BlockSpec((rows, d), lambda i: (i, 0)),
                  pl.BlockSpec((rows, d_main), lambda i: (i, 0)),
                  pl.BlockSpec((rows, d_mem), lambda i: (i, 0)),
                  pl.BlockSpec((None, d_main + d_mem, d), lambda i: (layer, 0, 0),
                               pipeline_mode=pl.Buffered(1)),
                  ln_spec, ln_spec],
        out_specs=[pl.BlockSpec((rows, d), lambda i: (i, 0))] * 2,
        out_shape=[jax.ShapeDtypeStruct((m, d), F32), jax.ShapeDtypeStruct((m, d), BF16)],
        compiler_params=_params(1),
        name="out_proj",
    )(xf, main, mo, w_out_b, ln_g, ln_b)


def _mlp_kernel(xb_ref, xf_ref, w1_ref, w2_ref, g_ref, b_ref, yf_ref, *yb_ref, alpha):
    f = pl.program_id(1)
    last = pl.num_programs(1) - 1

    def ff(rows):
        h = jnp.square(jnp.maximum(_dot(xb_ref[rows, :], w1_ref[...]), 0.0)).astype(BF16)
        return _dot(h, w2_ref[...])

    @pl.when(f == 0)
    def _():
        yf_ref[...] = ff(slice(None))

    @pl.when((f > 0) & (f < last))
    def _():
        yf_ref[...] += ff(slice(None))

    @pl.when(f == last)
    def _():
        for r in range(0, yf_ref.shape[0], MLP_EPILOGUE_ROWS):
            rows = slice(r, r + MLP_EPILOGUE_ROWS)
            acc = yf_ref[rows, :] + ff(rows)
            y = _layer_norm(alpha * xf_ref[rows, :] + acc, g_ref[...], b_ref[...])
            yf_ref[rows, :] = y
            for ref in yb_ref:
                ref[rows, :] = y.astype(BF16)


def _mlp(xb, xf, w_ff1_b, w_ff2_b, ln_g, ln_b, layer, alpha, *, want_bf16, rows=1024):
    m, d = xf.shape
    tf = WEIGHT_TILE
    n_f = w_ff1_b.shape[1] // tf
    ln_spec = pl.BlockSpec((None, 1, d), lambda i, f: (2 * layer + 1, 0, 0))
    row_spec = pl.BlockSpec((rows, d), lambda i, f: (i, 0))
    out_shape = [jax.ShapeDtypeStruct((m, d), F32)] + [jax.ShapeDtypeStruct((m, d), BF16)] * want_bf16
    out = pl.pallas_call(
        functools.partial(_mlp_kernel, alpha=alpha),
        grid=(m // rows, n_f),
        in_specs=[row_spec,
                  pl.BlockSpec((rows, d), lambda i, f: (i, 0), pipeline_mode=pl.Buffered(1)),
                  pl.BlockSpec((d, tf), lambda i, f: (0, f)),
                  pl.BlockSpec((tf, d), lambda i, f: (f, 0)),
                  ln_spec, ln_spec],
        out_specs=[row_spec] * len(out_shape),
        out_shape=out_shape,
        compiler_params=_params(2, MLP_VMEM_LIMIT_BYTES),
        name="mlp",
    )(xb, xf, w_ff1_b, w_ff2_b, ln_g, ln_b)
    return out if want_bf16 else (out[0], None)


def _inproj_b_kernel(x_ref, wk_ref, wv_ref, wq_ref, *rest, S, L, ride):
    (k_ref, v_ref, q_ref, qm_ref), cast_slab = _ride_along_cast(rest, 4, ride)
    j = pl.program_id(1)

    @pl.when(j > 0)
    def _():
        cast_slab()
        x = x_ref[...]
        for w_ref, o_ref in ((wk_ref, k_ref), (wv_ref, v_ref), (wq_ref, q_ref)):
            r = _dot(x, w_ref[...])
            for s in range(S):
                for h in range(o_ref.shape[1]):
                    o_ref[s, h] = r[s * L:(s + 1) * L,
                                    h * HEAD_DIM:(h + 1) * HEAD_DIM].astype(o_ref.dtype)

    @pl.when(j == 0)
    def _():
        cast_slab()
        qm_ref[...] = _dot(x_ref[...], wq_ref[...]).astype(BF16)


def _inproj_b(xb, w_kv_t, w_in_b_t, b_layer, d_main, n_seq, seq_len, ff_weights, ff_layer, *,
              rows=1024):
    m, d = xb.shape
    n_heads = d_main // HEAD_DIM
    tn = WEIGHT_TILE
    nct = d_main // tn
    assert w_in_b_t.shape[-1] == (nct + 1) * tn
    S, L, tps = _seq_tiling(n_seq, seq_len, rows)
    cj = lambda j: jnp.maximum(j - 1, 0)
    out_spec = pl.BlockSpec((S, tn // HEAD_DIM, L, HEAD_DIM),
                            lambda i, j: (i // tps, cj(j), i % tps, 0))
    shape = (n_seq, n_heads, seq_len, HEAD_DIM)
    grid = (m // rows, nct + 1)
    ride = ff_weights and _ride_along_specs(*ff_weights, ff_layer, grid)
    ride_in, ride_out, ride_shape = ride or ([], [], [])
    k, v, q, q_mem, *cast = pl.pallas_call(
        functools.partial(_inproj_b_kernel, S=S, L=L, ride=bool(ride)),
        grid=grid,
        in_specs=[pl.BlockSpec((rows, d), lambda i, j: (i, 0)),
                  pl.BlockSpec((d, tn), lambda i, j: (0, cj(j))),
                  pl.BlockSpec((d, tn), lambda i, j: (0, nct + cj(j))),
                  pl.BlockSpec((None, d, tn),
                               lambda i, j: (b_layer, 0, jnp.where(j == 0, nct, j - 1)))] + ride_in,
        out_specs=[out_spec] * 3 + [pl.BlockSpec((rows, tn), lambda i, j: (i, 0))] + ride_out,
        out_shape=[jax.ShapeDtypeStruct(shape, F32), jax.ShapeDtypeStruct(shape, F32),
                   jax.ShapeDtypeStruct(shape, BF16),
                   jax.ShapeDtypeStruct((m, tn), BF16)] + ride_shape,
        compiler_params=_params(2, RIDE_VMEM_LIMIT_BYTES if ride else VMEM_LIMIT_BYTES),
        name="inproj_b",
    )(xb, w_kv_t, w_kv_t, w_in_b_t, *(ff_weights if ride else ()))
    return k, v, q, q_mem, tuple(cast) or None


def _softplus(z):
    return jnp.maximum(z, 0.0) + jnp.log(1.0 + jnp.exp(-jnp.abs(z)))


def _sb_block(q_ref, k_ref, v_ref, ones_tri, carried_scr, out_scr, *, newest, rows=None):
    n_heads, qb, _ = q_ref.shape
    rows = qb if rows is None else rows
    kb = k_ref.shape[1]
    scale = 1.0 / math.sqrt(HEAD_DIM)
    stack = lambda per_head: jnp.concatenate([per_head(h) for h in range(n_heads)], axis=0)
    z = stack(lambda h: lax.dot_general(q_ref[h, :rows], k_ref[h].astype(BF16), _NT,
                                        preferred_element_type=F32)) * scale
    sp = _softplus(z)
    log_a = z - sp
    if newest:
        row = lax.broadcasted_iota(jnp.int32, z.shape, 0) & (qb - 1)
        col = lax.broadcasted_iota(jnp.int32, z.shape, 1)
        valid = col < row
        sp = jnp.where(valid, sp, 0.0)
    hi = sp.astype(BF16)
    lo = (sp - hi.astype(F32)).astype(BF16)
    if ones_tri.shape[0] == 2 * kb:
        sums = _dot(jnp.concatenate([hi, lo], axis=1), ones_tri)
    else:
        sums = _dot(hi, ones_tri) + _dot(lo, ones_tri)
    log_a = log_a - sums[:, HEAD_DIM:HEAD_DIM + kb]
    if not newest:
        log_a = log_a - stack(lambda h: carried_scr[h, :rows, :kb])
    a = jnp.exp(log_a)
    if newest:
        a = jnp.where(valid, a, 0.0)
    a = a.astype(BF16)
    for h in range(n_heads):
        head = slice(h * rows, (h + 1) * rows)
        contrib = _dot(a[head], v_ref[h].astype(BF16))
        if newest:
            out_scr[h] = contrib
            carried_scr[h] = sums[head, :HEAD_DIM]
        else:
            out_scr[h, :rows] += contrib
            carried_scr[h, :rows] += sums[head, :HEAD_DIM]


def _sb_kernel(q_ref, kd_ref, vd_ref, k1_ref, v1_ref, k2_ref, v2_ref, otd_ref, otp_ref,
               k_hbm, v_hbm, o_ref, carried_scr, out_scr, k_buf, v_buf, sem, *, n_past_static):
    b = pl.program_id(0)
    n_past = pl.program_id(1) if n_past_static is None else n_past_static
    qb = q_ref.shape[1]
    tail = min(SB_TAIL_ROWS, qb)

    def alive():
        mins = [carried_scr[h] for h in range(carried_scr.shape[0])]
        while len(mins) > 1:
            mins = [functools.reduce(jnp.minimum, mins[i:i + 2]) for i in range(0, len(mins), 2)]
        tail_alive = jnp.min(mins[0][tail:]) <= F32_EXP_UNDERFLOW if tail < qb else False
        return tail_alive | (jnp.min(mins[0][:tail]) <= F32_EXP_UNDERFLOW), tail_alive

    def past_block(k_ref, v_ref, rows=None):
        _sb_block(q_ref, k_ref, v_ref, otp_ref[...], carried_scr, out_scr, newest=False, rows=rows)

    _sb_block(q_ref, kd_ref, vd_ref, otd_ref[...], carried_scr, out_scr, newest=True)

    @pl.when(n_past >= 1)
    def _():
        past_block(k1_ref, v1_ref)

    any_alive, tail_alive = alive()

    @pl.when((n_past >= 2) & tail_alive)
    def _():
        past_block(k2_ref, v2_ref)

    @pl.when((n_past >= 2) & any_alive & jnp.logical_not(tail_alive))
    def _():
        past_block(k2_ref, v2_ref, rows=tail)

    n_left = n_past - 2

    def fetch(hbm, buf, slot, blk):
        start = pl.multiple_of(blk * K_BLOCK, K_BLOCK)
        return pltpu.make_async_copy(hbm.at[b, :, pl.ds(start, K_BLOCK), :], buf, sem.at[slot])

    def body(carry):
        i, _ = carry
        blk = n_left - 1 - i
        fetch(k_hbm, k_buf, 0, blk).start()
        fetch(v_hbm, v_buf, 1, blk).start()
        fetch(k_hbm, k_buf, 0, blk).wait()
        fetch(v_hbm, v_buf, 1, blk).wait()
        past_block(k_buf, v_buf)
        return i + 1, alive()[0]

    @pl.when(n_left > 0)
    def _():
        lax.while_loop(lambda c: (c[0] < n_left) & c[1], body, (jnp.int32(0), alive()[0]))

    for h in range(out_scr.shape[0]):
        o_ref[:, h * HEAD_DIM:(h + 1) * HEAD_DIM] = out_scr[h].astype(o_ref.dtype)


def _ones_tri(kb):
    tri = np.tril(np.ones((kb, kb), np.float32), -1)
    m = np.concatenate([np.ones((kb, HEAD_DIM), np.float32), tri], axis=1)
    if kb % 128 == 0:
        m = np.concatenate([m, m], axis=0)
    return jnp.asarray(m, BF16)


def _stick_breaking(q, k_new, v_new, k_past, v_past):
    n_seq, n_heads, seq_len, _ = q.shape
    c = n_heads * HEAD_DIM
    qb = min(K_BLOCK, seq_len)
    assert qb & (qb - 1) == 0
    nq = seq_len // qb
    if k_past is None:
        assert qb == K_BLOCK
        k_past, v_past, past_blocks, n_past_static = k_new, v_new, nq, None
    else:
        assert nq == 1 and k_past.shape[2] % K_BLOCK == 0
        past_blocks = k_past.shape[2] // K_BLOCK
        n_past_static = past_blocks

    def past_spec(back):
        def index_map(b, i):
            n_past = i if n_past_static is None else n_past_static
            return (b, 0, jnp.clip(n_past - back, 0, past_blocks - 1), 0)
        return pl.BlockSpec((None, n_heads, K_BLOCK, HEAD_DIM), index_map)

    new_spec = pl.BlockSpec((None, n_heads, qb, HEAD_DIM), lambda b, i: (b, 0, i, 0))
    hbm_spec = pl.BlockSpec(memory_space=pl.ANY)
    otd, otp = _ones_tri(qb), _ones_tri(K_BLOCK)
    return pl.pallas_call(
        functools.partial(_sb_kernel, n_past_static=n_past_static),
        grid=(n_seq, nq),
        in_specs=[new_spec, new_spec, new_spec,
                  past_spec(1), past_spec(1), past_spec(2), past_spec(2),
                  pl.BlockSpec(otd.shape, lambda b, i: (0, 0)),
                  pl.BlockSpec(otp.shape, lambda b, i: (0, 0)),
                  hbm_spec, hbm_spec],
        out_specs=pl.BlockSpec((qb, c), lambda b, i: (b * nq + i, 0)),
        out_shape=jax.ShapeDtypeStruct((n_seq * seq_len, c), BF16),
        scratch_shapes=[pltpu.VMEM((n_heads, qb, HEAD_DIM), F32),
                        pltpu.VMEM((n_heads, qb, HEAD_DIM), F32),
                        pltpu.VMEM((n_heads, K_BLOCK, HEAD_DIM), k_past.dtype),
                        pltpu.VMEM((n_heads, K_BLOCK, HEAD_DIM), v_past.dtype),
                        pltpu.SemaphoreType.DMA((2,))],
        compiler_params=_params(2),
        name="stick_breaking",
    )(q, k_new, v_new, k_past, v_past, k_past, v_past, otd, otp, k_past, v_past)


def _trunk(x, conv_state, k_past, v_past, mem_k, mem_v, w, ff_bf16, *, attn_rows):
    n_seq, seq_len, d = x.shape
    depth, n_a, d_main, alpha = w["depth"], w["n_a"], w["d_main"], w["alpha"]
    xf = x.reshape(n_seq * seq_len, d)
    xb = xf
    conv_states = []
    k_new = v_new = None
    for layer in range(depth):
        ff_f32 = None if layer in ff_bf16 else (w["w_ff1"], w["w_ff2"])
        if layer < n_a:
            main, st, q_mem, cast = _inproj_a(xb, conv_state, w["w_in_a"], w["conv_w_t"], layer,
                                              n_seq, seq_len, ff_f32)
            conv_states.append(st)
        else:
            k, v, q, q_mem, cast = _inproj_b(xb, w["w_kv"], w["w_in_b"], layer - n_a, d_main,
                                             n_seq, seq_len, ff_f32, layer)
            if k_new is None:
                k_new, v_new = k, v
            kp = None if k_past is None else jnp.transpose(k_past, (0, 2, 1, 3))
            vp = None if v_past is None else jnp.transpose(v_past, (0, 2, 1, 3))
            main = _stick_breaking(q, k_new, v_new, kp, vp)
        mo = _mem_attn(q_mem, mem_k, mem_v, layer, n_seq, seq_len, rows=attn_rows)
        xf, xb = _out_proj(xf, main, mo, w["w_out"], w["ln_g"], w["ln_b"], layer, alpha)
        if layer not in ff_bf16:
            ff_bf16[layer] = cast or (w["w_ff1"][layer].astype(BF16), w["w_ff2"][layer].astype(BF16))
        xf, xb = _mlp(xb, xf, *ff_bf16[layer], w["ln_g"], w["ln_b"], layer, alpha,
                      want_bf16=layer + 1 < depth)
    return (xf.reshape(n_seq, seq_len, d), jnp.stack(conv_states),
            jnp.transpose(k_new, (0, 2, 1, 3)), jnp.transpose(v_new, (0, 2, 1, 3)))


def kernel(x_prompt, x_sample, state_conv, cache_k, cache_v, cache_mem_k, cache_mem_v, mem_prompt,
           w_in_a, conv_w, w_in_b, w_kv, w_mem_kv, w_out, w_ff1, w_ff2, ln_g, ln_b):
    depth, d_mix, d = w_out.shape
    n_a, d_main = conv_w.shape[0], conv_w.shape[1]
    d_mem = d_mix - d_main
    n_mem_heads = d_mem // HEAD_DIM
    n_prompt, n_mem = mem_prompt.shape[0], mem_prompt.shape[1]
    w = dict(
        depth=depth, n_a=n_a, d_main=d_main, alpha=(2.0 * depth) ** 0.25,
        w_in_a=w_in_a.astype(BF16), conv_w_t=jnp.swapaxes(conv_w, 1, 2),
        w_in_b=w_in_b.astype(BF16), w_kv=w_kv.astype(BF16), w_out=w_out.astype(BF16),
        w_ff1=w_ff1, w_ff2=w_ff2,
        ln_g=ln_g.reshape(depth * 2, 1, d), ln_b=ln_b.reshape(depth * 2, 1, d))

    mem_k_p, mem_v_p = _memkv(mem_prompt.reshape(n_prompt * n_mem, d), w_mem_kv)
    rows_view = lambda a, b: a.reshape(depth, b, n_mem * n_mem_heads, HEAD_DIM)
    mem_k_p, mem_v_p = rows_view(mem_k_p, n_prompt), rows_view(mem_v_p, n_prompt)
    ff_bf16 = {}
    y_p, conv_p, k_p, v_p = _trunk(x_prompt, None, None, None, mem_k_p, mem_v_p, w, ff_bf16,
                                   attn_rows=1024)
    n_sample = x_sample.shape[0]
    y_s, conv_s, k_s, v_s = _trunk(x_sample, state_conv, cache_k, cache_v,
                                   rows_view(cache_mem_k, n_sample), rows_view(cache_mem_v, n_sample),
                                   w, ff_bf16, attn_rows=512)
    mem_shape = (depth, n_prompt, n_mem, n_mem_heads, HEAD_DIM)
    return (y_p, y_s, conv_p, conv_s, k_p, v_p, k_s, v_s,
            mem_k_p.reshape(mem_shape), mem_v_p.reshape(mem_shape))
```

```python
import functools
import math

import jax
import jax.numpy as jnp
import numpy as np
from jax import lax
from jax.experimental import pallas as pl
from jax.experimental.pallas import tpu as pltpu

F32 = jnp.float32
BF16 = jnp.bfloat16

HEAD_DIM = 128
LN_EPS = 1e-5
K_BLOCK = 128
SB_TAIL_ROWS = 32
HALO_ROWS = 8
F32_EXP_UNDERFLOW = 104.0
WEIGHT_TILE = 512
EPILOGUE_ROWS = 128
CONV_CHUNK_ROWS = 256
MLP_EPILOGUE_ROWS = 256
VMEM_LIMIT_BYTES = 48 * 1024 * 1024
RIDE_VMEM_LIMIT_BYTES = 56 * 1024 * 1024
MLP_VMEM_LIMIT_BYTES = 58 * 1024 * 1024

_NT = (((1,), (1,)), ((), ()))


def _dot(a, b):
    return jnp.dot(a, b, preferred_element_type=F32)


def _params(grid_rank, vmem_limit_bytes=VMEM_LIMIT_BYTES):
    return pltpu.CompilerParams(dimension_semantics=("arbitrary",) * grid_rank,
                                vmem_limit_bytes=vmem_limit_bytes)


def _layer_norm(y, g, b):
    mu = jnp.mean(y, axis=-1, keepdims=True)
    d = y - mu
    var = jnp.mean(d * d, axis=-1, keepdims=True)
    return d * lax.rsqrt(var + LN_EPS) * g + b


def _seq_tiling(n_seq, seq_len, rows):
    if seq_len >= rows:
        assert seq_len % rows == 0
        return 1, rows, seq_len // rows
    assert rows % seq_len == 0 and n_seq % (rows // seq_len) == 0
    return rows // seq_len, seq_len, 1


def _memkv_kernel(mem_ref, w_ref, k_ref, v_ref, *, n_heads):
    rows = mem_ref.shape[0]
    r = _dot(mem_ref[...].astype(BF16), w_ref[...].astype(BF16))
    for i, ref in enumerate((k_ref, v_ref)):
        for h in range(n_heads):
            col = (i * n_heads + h) * HEAD_DIM
            ref[pl.ds(h, rows, stride=n_heads), :] = r[:, col:col + HEAD_DIM]


def _memkv(mem, w_mem_kv):
    rows, d = mem.shape
    depth, _, two_dmem = w_mem_kv.shape
    n_heads = two_dmem // 2 // HEAD_DIM
    out = jax.ShapeDtypeStruct((depth, rows * n_heads, HEAD_DIM), F32)
    return pl.pallas_call(
        functools.partial(_memkv_kernel, n_heads=n_heads),
        grid=(depth,),
        in_specs=[pl.BlockSpec((rows, d), lambda l: (0, 0)),
                  pl.BlockSpec((None, d, two_dmem), lambda l: (l, 0, 0))],
        out_specs=[pl.BlockSpec((None, rows * n_heads, HEAD_DIM), lambda l: (l, 0, 0))] * 2,
        out_shape=[out, out],
        compiler_params=_params(1),
        name="memkv",
    )(mem, w_mem_kv)


def _ride_along_specs(w_ff1, w_ff2, layer, grid):
    n_steps = grid[0] * grid[1]
    _, d, d_ff = w_ff1.shape
    if d_ff % (n_steps * 128):
        return None
    slab = d_ff // n_steps
    step = lambda i, j: i * grid[1] + j
    in_specs = [pl.BlockSpec((None, d, slab), lambda i, j: (layer, 0, step(i, j))),
                pl.BlockSpec((None, slab, d), lambda i, j: (layer, step(i, j), 0))]
    out_specs = [pl.BlockSpec((d, slab), lambda i, j: (0, step(i, j))),
                 pl.BlockSpec((slab, d), lambda i, j: (step(i, j), 0))]
    out_shape = [jax.ShapeDtypeStruct((d, d_ff), BF16), jax.ShapeDtypeStruct((d_ff, d), BF16)]
    return in_specs, out_specs, out_shape


def _ride_along_cast(rest, n_out, ride):
    if not ride:
        return rest, lambda: None

    def cast():
        rest[2 + n_out][...] = rest[0][...].astype(BF16)
        rest[3 + n_out][...] = rest[1][...].astype(BF16)

    return rest[2:2 + n_out] + rest[4 + n_out:], cast


def _inproj_a_kernel(x_ref, state_ref, wx_ref, wb_ref, wc_ref, cw_ref, wqm_ref, *rest,
                     S, L, tiles_per_seq, ride):
    (main_ref, st_ref, qm_ref, u_scr, carry_scr), cast_slab = _ride_along_cast(rest, 3, ride)
    j = pl.program_id(1)

    @pl.when(j > 0)
    def _():
        cast_slab()
        wx = wx_ref[...]
        wb = wb_ref[...]
        wc = wc_ref[...]
        cw = cw_ref[...]
        w0, w1, w2 = cw[0:1], cw[1:2], cw[2:3]
        prev_rows = slice(HALO_ROWS - 2, HALO_ROWS)
        if tiles_per_seq == 1:
            for s in range(S):
                u_scr[s, prev_rows, :] = state_ref[s]
        else:
            first = (pl.program_id(0) % tiles_per_seq) == 0

            @pl.when(first)
            def _():
                u_scr[0, prev_rows, :] = state_ref[0]

            @pl.when(jnp.logical_not(first))
            def _():
                u_scr[0, prev_rows, :] = carry_scr[j - 1]
        piece = min(CONV_CHUNK_ROWS, L)
        assert CONV_CHUNK_ROWS % piece == 0 and L % piece == 0 and (S * L) % CONV_CHUNK_ROWS == 0
        for r in range(0, S * L, CONV_CHUNK_ROWS):
            x = x_ref[r:r + CONV_CHUNK_ROWS, :].astype(BF16)
            u = _dot(x, wc) * _dot(x, wx)
            gate_b = _dot(x, wb)
            for p in range(0, CONV_CHUNK_ROWS, piece):
                s, t0 = divmod(r + p, L)
                lo = HALO_ROWS + t0
                u_scr[s, lo:lo + piece, :] = u[p:p + piece]
                y = u_scr[s, lo - 2:lo - 2 + piece, :] * w0
                y = y + u_scr[s, lo - 1:lo - 1 + piece, :] * w1
                y = y + u[p:p + piece] * w2
                main_ref[r + p:r + p + piece, :] = (gate_b[p:p + piece] * y).astype(main_ref.dtype)
                if t0 + piece == L:
                    st_ref[s] = u_scr[s, HALO_ROWS + L - 2:HALO_ROWS + L, :]
        if tiles_per_seq > 1:
            carry_scr[j - 1] = u_scr[0, HALO_ROWS + L - 2:HALO_ROWS + L, :]

    @pl.when(j == 0)
    def _():
        cast_slab()
        qm_ref[...] = _dot(x_ref[...].astype(BF16), wqm_ref[...]).astype(BF16)


def _inproj_a(x, state, w_in_a_t, conv_w_t, layer, n_seq, seq_len, ff_weights, *, rows=1024):
    m, d = x.shape
    c = conv_w_t.shape[-1]
    tn = WEIGHT_TILE
    S, L, tps = _seq_tiling(n_seq, seq_len, rows)
    nct = c // tn
    cj = lambda j: jnp.maximum(j - 1, 0)
    grid = (m // rows, nct + 1)
    ride = ff_weights and _ride_along_specs(*ff_weights, layer, grid)
    ride_in, ride_out, ride_shape = ride or ([], [], [])
    kern = functools.partial(_inproj_a_kernel, S=S, L=L, tiles_per_seq=tps, ride=bool(ride))
    w_spec = lambda off: pl.BlockSpec((None, d, tn), lambda i, j: (layer, 0, off + cj(j)))
    main, tile_state, q_mem, *cast = pl.pallas_call(
        kern,
        grid=grid,
        in_specs=[pl.BlockSpec((rows, d), lambda i, j: (i, 0)),
                  pl.BlockSpec((None, S, 2, tn), lambda i, j: (layer, i // tps, 0, cj(j))),
                  w_spec(0), w_spec(nct), w_spec(2 * nct),
                  pl.BlockSpec((None, 3, tn), lambda i, j: (layer, 0, cj(j))),
                  pl.BlockSpec((None, d, tn), lambda i, j: (layer, 0, 3 * nct))] + ride_in,
        out_specs=[pl.BlockSpec((rows, tn), lambda i, j: (i, cj(j))),
                   pl.BlockSpec((S, 2, tn), lambda i, j: (i, 0, cj(j))),
                   pl.BlockSpec((rows, tn), lambda i, j: (i, 0))] + ride_out,
        out_shape=[jax.ShapeDtypeStruct((m, c), BF16),
                   jax.ShapeDtypeStruct((n_seq * tps, 2, c), F32),
                   jax.ShapeDtypeStruct((m, tn), BF16)] + ride_shape,
        scratch_shapes=[pltpu.VMEM((S, HALO_ROWS + L, tn), F32), pltpu.VMEM((nct, 2, tn), F32)],
        compiler_params=_params(2, RIDE_VMEM_LIMIT_BYTES if ride else VMEM_LIMIT_BYTES),
        name="inproj_a",
    )(x, state, w_in_a_t, w_in_a_t, w_in_a_t, conv_w_t, w_in_a_t, *(ff_weights if ride else ()))
    return main, tile_state.reshape(n_seq, tps, 2, c)[:, tps - 1], q_mem, tuple(cast) or None


def _mem_attn_kernel(q_ref, mk_ref, mv_ref, o_ref, *, S, L):
    n_heads = q_ref.shape[1] // HEAD_DIM
    n_mem = mk_ref.shape[1] // n_heads
    scale = 1.0 / math.sqrt(HEAD_DIM)
    tiles = [(slice(s * L, (s + 1) * L), slice(h * HEAD_DIM, (h + 1) * HEAD_DIM),
              s, pl.ds(h, n_mem, stride=n_heads))
             for s in range(S) for h in range(n_heads)]
    sc = jnp.concatenate(
        [lax.dot_general(q_ref[rows, cols], mk_ref[s, head_rows, :].astype(BF16), _NT,
                         preferred_element_type=F32) for rows, cols, s, head_rows in tiles],
        axis=0) * scale
    e = jnp.exp(sc - jnp.max(sc, axis=-1, keepdims=True))
    p = (e / jnp.sum(e, axis=-1, keepdims=True)).astype(BF16)
    for i, (rows, cols, s, head_rows) in enumerate(tiles):
        v = mv_ref[s, head_rows, :].astype(BF16)
        o_ref[rows, cols] = _dot(p[i * L:(i + 1) * L], v).astype(BF16)


def _mem_attn(q_mem, mem_k, mem_v, layer, n_seq, seq_len, *, rows):
    m, d_mem = q_mem.shape
    S, L, tps = _seq_tiling(n_seq, seq_len, rows)
    mem_spec = pl.BlockSpec((None, S) + mem_k.shape[2:], lambda i: (layer, i // tps, 0, 0))
    row_spec = pl.BlockSpec((rows, d_mem), lambda i: (i, 0))
    return pl.pallas_call(
        functools.partial(_mem_attn_kernel, S=S, L=L),
        grid=(m // rows,),
        in_specs=[row_spec, mem_spec, mem_spec],
        out_specs=row_spec,
        out_shape=jax.ShapeDtypeStruct((m, d_mem), BF16),
        compiler_params=_params(1),
        name="mem_attn",
    )(q_mem, mem_k, mem_v)


def _out_proj_kernel(x_ref, main_ref, mo_ref, wo_ref, g_ref, b_ref, yf_ref, yb_ref, *, alpha):
    d_main = main_ref.shape[1]
    for r in range(0, x_ref.shape[0], EPILOGUE_ROWS):
        rows = slice(r, r + EPILOGUE_ROWS)
        mix = (_dot(main_ref[rows, :], wo_ref[0:d_main, :])
               + _dot(mo_ref[rows, :], wo_ref[d_main:, :]))
        y = _layer_norm(alpha * x_ref[rows, :] + mix, g_ref[...], b_ref[...])
        yf_ref[rows, :] = y
        yb_ref[rows, :] = y.astype(BF16)


def _out_proj(xf, main, mo, w_out_b, ln_g, ln_b, layer, alpha, *, rows=512):
    m, d = xf.shape
    d_main, d_mem = main.shape[1], mo.shape[1]
    ln_spec = pl.BlockSpec((None, 1, d), lambda i: (2 * layer, 0, 0))
    return pl.pallas_call(
        functools.partial(_out_proj_kernel, alpha=alpha),
        grid=(m // rows,),
        in_specs=[pl.BlockSpec((rows, d), lambda i: (i, 0)),
                  pl.BlockSpec((rows, d_main), lambda i: (i, 0)),
                  pl.BlockSpec((rows, d_mem), lambda i: (i, 0)),
                  pl.BlockSpec((None, d_main + d_mem, d), lambda i: (layer, 0, 0),
                               pipeline_mode=pl.Buffered(1)),
                  ln_spec, ln_spec],
        out_specs=[pl.BlockSpec((rows, d), lambda i: (i, 0))] * 2,
        out_shape=[jax.ShapeDtypeStruct((m, d), F32), jax.ShapeDtypeStruct((m, d), BF16)],
        compiler_params=_params(1),
        name="out_proj",
    )(xf, main, mo, w_out_b, ln_g, ln_b)


def _mlp_kernel(xb_ref, xf_hbm, w1_ref, w2_ref, g_ref, b_ref, yf_ref, *rest, alpha):
    *yb_ref, xf_buf, xf_sem = rest
    f = pl.program_id(1)
    last = pl.num_programs(1) - 1
    n_rows = yf_ref.shape[0]

    def residual_copy():
        start = pl.multiple_of(pl.program_id(0) * n_rows, n_rows)
        return pltpu.make_async_copy(xf_hbm.at[pl.ds(start, n_rows), :], xf_buf, xf_sem)

    def ff(rows):
        h = jnp.square(jnp.maximum(_dot(xb_ref[rows, :], w1_ref[...]), 0.0)).astype(BF16)
        return _dot(h, w2_ref[...])

    @pl.when(f == 0)
    def _():
        residual_copy().start()
        yf_ref[...] = ff(slice(None))

    @pl.when((f > 0) & (f < last))
    def _():
        yf_ref[...] += ff(slice(None))

    @pl.when(f == last)
    def _():
        residual_copy().wait()
        for r in range(0, n_rows, MLP_EPILOGUE_ROWS):
            rows = slice(r, r + MLP_EPILOGUE_ROWS)
            acc = yf_ref[rows, :] + ff(rows)
            y = _layer_norm(alpha * xf_buf[rows, :] + acc, g_ref[...], b_ref[...])
            yf_ref[rows, :] = y
            for ref in yb_ref:
                ref[rows, :] = y.astype(BF16)


def _mlp(xb, xf, w_ff1_b, w_ff2_b, ln_g, ln_b, layer, alpha, *, want_bf16, rows=1024):
    m, d = xf.shape
    tf = WEIGHT_TILE
    n_f = w_ff1_b.shape[1] // tf
    assert n_f >= 2 and m % rows == 0
    ln_spec = pl.BlockSpec((None, 1, d), lambda i, f: (2 * layer + 1, 0, 0))
    row_spec = pl.BlockSpec((rows, d), lambda i, f: (i, 0))
    out_shape = [jax.ShapeDtypeStruct((m, d), F32)] + [jax.ShapeDtypeStruct((m, d), BF16)] * want_bf16
    out = pl.pallas_call(
        functools.partial(_mlp_kernel, alpha=alpha),
        grid=(m // rows, n_f),
        in_specs=[row_spec,
                  pl.BlockSpec(memory_space=pl.ANY),
                  pl.BlockSpec((d, tf), lambda i, f: (0, f)),
                  pl.BlockSpec((tf, d), lambda i, f: (f, 0)),
                  ln_spec, ln_spec],
        out_specs=[row_spec] * len(out_shape),
        out_shape=out_shape,
        scratch_shapes=[pltpu.VMEM((rows, d), F32), pltpu.SemaphoreType.DMA(())],
        compiler_params=_params(2, MLP_VMEM_LIMIT_BYTES),
        name="mlp",
    )(xb, xf, w_ff1_b, w_ff2_b, ln_g, ln_b)
    return out if want_bf16 else (out[0], None)


def _inproj_b_kernel(x_ref, wk_ref, wv_ref, wq_ref, *rest, S, L, ride):
    (k_ref, v_ref, q_ref, qm_ref), cast_slab = _ride_along_cast(rest, 4, ride)
    j = pl.program_id(1)

    @pl.when(j > 0)
    def _():
        cast_slab()
        x = x_ref[...]
        for w_ref, o_ref in ((wk_ref, k_ref), (wv_ref, v_ref), (wq_ref, q_ref)):
            r = _dot(x, w_ref[...])
            for s in range(S):
                for h in range(o_ref.shape[1]):
                    o_ref[s, h] = r[s * L:(s + 1) * L,
                                    h * HEAD_DIM:(h + 1) * HEAD_DIM].astype(o_ref.dtype)

    @pl.when(j == 0)
    def _():
        cast_slab()
        qm_ref[...] = _dot(x_ref[...], wq_ref[...]).astype(BF16)


def _inproj_b(xb, w_kv_t, w_in_b_t, b_layer, d_main, n_seq, seq_len, ff_weights, ff_layer, *,
              rows=1024):
    m, d = xb.shape
    n_heads = d_main // HEAD_DIM
    tn = WEIGHT_TILE
    nct = d_main // tn
    assert w_in_b_t.shape[-1] == (nct + 1) * tn
    S, L, tps = _seq_tiling(n_seq, seq_len, rows)
    cj = lambda j: jnp.maximum(j - 1, 0)
    out_spec = pl.BlockSpec((S, tn // HEAD_DIM, L, HEAD_DIM),
                            lambda i, j: (i // tps, cj(j), i % tps, 0))
    shape = (n_seq, n_heads, seq_len, HEAD_DIM)
    grid = (m // rows, nct + 1)
    ride = ff_weights and _ride_along_specs(*ff_weights, ff_layer, grid)
    ride_in, ride_out, ride_shape = ride or ([], [], [])
    k, v, q, q_mem, *cast = pl.pallas_call(
        functools.partial(_inproj_b_kernel, S=S, L=L, ride=bool(ride)),
        grid=grid,
        in_specs=[pl.BlockSpec((rows, d), lambda i, j: (i, 0)),
                  pl.BlockSpec((d, tn), lambda i, j: (0, cj(j))),
                  pl.BlockSpec((d, tn), lambda i, j: (0, nct + cj(j))),
                  pl.BlockSpec((None, d, tn),
                               lambda i, j: (b_layer, 0, jnp.where(j == 0, nct, j - 1)))] + ride_in,
        out_specs=[out_spec] * 3 + [pl.BlockSpec((rows, tn), lambda i, j: (i, 0))] + ride_out,
        out_shape=[jax.ShapeDtypeStruct(shape, F32), jax.ShapeDtypeStruct(shape, F32),
                   jax.ShapeDtypeStruct(shape, BF16),
                   jax.ShapeDtypeStruct((m, tn), BF16)] + ride_shape,
        compiler_params=_params(2, RIDE_VMEM_LIMIT_BYTES if ride else VMEM_LIMIT_BYTES),
        name="inproj_b",
    )(xb, w_kv_t, w_kv_t, w_in_b_t, *(ff_weights if ride else ()))
    return k, v, q, q_mem, tuple(cast) or None


def _softplus(z):
    return jnp.maximum(z, 0.0) + jnp.log(1.0 + jnp.exp(-jnp.abs(z)))


def _sb_block(q_ref, k_ref, v_ref, ones_tri, carried_scr, out_scr, *, newest, rows=None):
    n_heads, qb, _ = q_ref.shape
    rows = qb if rows is None else rows
    kb = k_ref.shape[1]
    scale = 1.0 / math.sqrt(HEAD_DIM)
    stack = lambda per_head: jnp.concatenate([per_head(h) for h in range(n_heads)], axis=0)
    z = stack(lambda h: lax.dot_general(q_ref[h, :rows], k_ref[h].astype(BF16), _NT,
                                        preferred_element_type=F32)) * scale
    sp = _softplus(z)
    log_a = z - sp
    if newest:
        row = lax.broadcasted_iota(jnp.int32, z.shape, 0) & (qb - 1)
        col = lax.broadcasted_iota(jnp.int32, z.shape, 1)
        valid = col < row
        sp = jnp.where(valid, sp, 0.0)
    hi = sp.astype(BF16)
    lo = (sp - hi.astype(F32)).astype(BF16)
    if ones_tri.shape[0] == 2 * kb:
        sums = _dot(jnp.concatenate([hi, lo], axis=1), ones_tri)
    else:
        sums = _dot(hi, ones_tri) + _dot(lo, ones_tri)
    log_a = log_a - sums[:, HEAD_DIM:HEAD_DIM + kb]
    if not newest:
        log_a = log_a - stack(lambda h: carried_scr[h, :rows, :kb])
    a = jnp.exp(log_a)
    if newest:
        a = jnp.where(valid, a, 0.0)
    a = a.astype(BF16)
    for h in range(n_heads):
        head = slice(h * rows, (h + 1) * rows)
        contrib = _dot(a[head], v_ref[h].astype(BF16))
        if newest:
            out_scr[h] = contrib
            carried_scr[h] = sums[head, :HEAD_DIM]
        else:
            out_scr[h, :rows] += contrib
            carried_scr[h, :rows] += sums[head, :HEAD_DIM]


def _sb_kernel(q_ref, kd_ref, vd_ref, k1_ref, v1_ref, k2_ref, v2_ref, otd_ref, otp_ref,
               k_hbm, v_hbm, o_ref, carried_scr, out_scr, k_buf, v_buf, sem, *, n_past_static):
    b = pl.program_id(0)
    n_past = pl.program_id(1) if n_past_static is None else n_past_static
    qb = q_ref.shape[1]
    tail = min(SB_TAIL_ROWS, qb)

    def alive():
        mins = [carried_scr[h] for h in range(carried_scr.shape[0])]
        while len(mins) > 1:
            mins = [functools.reduce(jnp.minimum, mins[i:i + 2]) for i in range(0, len(mins), 2)]
        tail_alive = jnp.min(mins[0][tail:]) <= F32_EXP_UNDERFLOW if tail < qb else False
        return tail_alive | (jnp.min(mins[0][:tail]) <= F32_EXP_UNDERFLOW), tail_alive

    def past_block(k_ref, v_ref, rows=None):
        _sb_block(q_ref, k_ref, v_ref, otp_ref[...], carried_scr, out_scr, newest=False, rows=rows)

    _sb_block(q_ref, kd_ref, vd_ref, otd_ref[...], carried_scr, out_scr, newest=True)

    @pl.when(n_past >= 1)
    def _():
        past_block(k1_ref, v1_ref)

    any_alive, tail_alive = alive()

    @pl.when((n_past >= 2) & tail_alive)
    def _():
        past_block(k2_ref, v2_ref)

    @pl.when((n_past >= 2) & any_alive & jnp.logical_not(tail_alive))
    def _():
        past_block(k2_ref, v2_ref, rows=tail)

    n_left = n_past - 2

    def fetch(hbm, buf, slot, blk):
        start = pl.multiple_of(blk * K_BLOCK, K_BLOCK)
        return pltpu.make_async_copy(hbm.at[b, :, pl.ds(start, K_BLOCK), :], buf, sem.at[slot])

    def body(carry):
        i, _ = carry
        blk = n_left - 1 - i
        fetch(k_hbm, k_buf, 0, blk).start()
        fetch(v_hbm, v_buf, 1, blk).start()
        fetch(k_hbm, k_buf, 0, blk).wait()
        fetch(v_hbm, v_buf, 1, blk).wait()
        past_block(k_buf, v_buf)
        return i + 1, alive()[0]

    @pl.when(n_left > 0)
    def _():
        lax.while_loop(lambda c: (c[0] < n_left) & c[1], body, (jnp.int32(0), alive()[0]))

    for h in range(out_scr.shape[0]):
        o_ref[:, h * HEAD_DIM:(h + 1) * HEAD_DIM] = out_scr[h].astype(o_ref.dtype)


def _ones_tri(kb):
    tri = np.tril(np.ones((kb, kb), np.float32), -1)
    m = np.concatenate([np.ones((kb, HEAD_DIM), np.float32), tri], axis=1)
    if kb % 128 == 0:
        m = np.concatenate([m, m], axis=0)
    return jnp.asarray(m, BF16)


def _stick_breaking(q, k_new, v_new, k_past, v_past):
    n_seq, n_heads, seq_len, _ = q.shape
    c = n_heads * HEAD_DIM
    qb = min(K_BLOCK, seq_len)
    assert qb & (qb - 1) == 0
    nq = seq_len // qb
    if k_past is None:
        assert qb == K_BLOCK
        k_past, v_past, past_blocks, n_past_static = k_new, v_new, nq, None
    else:
        assert nq == 1 and k_past.shape[2] % K_BLOCK == 0
        past_blocks = k_past.shape[2] // K_BLOCK
        n_past_static = past_blocks

    def past_spec(back):
        def index_map(b, i):
            n_past = i if n_past_static is None else n_past_static
            return (b, 0, jnp.clip(n_past - back, 0, past_blocks - 1), 0)
        return pl.BlockSpec((None, n_heads, K_BLOCK, HEAD_DIM), index_map)

    new_spec = pl.BlockSpec((None, n_heads, qb, HEAD_DIM), lambda b, i: (b, 0, i, 0))
    hbm_spec = pl.BlockSpec(memory_space=pl.ANY)
    otd, otp = _ones_tri(qb), _ones_tri(K_BLOCK)
    return pl.pallas_call(
        functools.partial(_sb_kernel, n_past_static=n_past_static),
        grid=(n_seq, nq),
        in_specs=[new_spec, new_spec, new_spec,
                  past_spec(1), past_spec(1), past_spec(2), past_spec(2),
                  pl.BlockSpec(otd.shape, lambda b, i: (0, 0)),
                  pl.BlockSpec(otp.shape, lambda b, i: (0, 0)),
                  hbm_spec, hbm_spec],
        out_specs=pl.BlockSpec((qb, c), lambda b, i: (b * nq + i, 0)),
        out_shape=jax.ShapeDtypeStruct((n_seq * seq_len, c), BF16),
        scratch_shapes=[pltpu.VMEM((n_heads, qb, HEAD_DIM), F32),
                        pltpu.VMEM((n_heads, qb, HEAD_DIM), F32),
                        pltpu.VMEM((n_heads, K_BLOCK, HEAD_DIM), k_past.dtype),
                        pltpu.VMEM((n_heads, K_BLOCK, HEAD_DIM), v_past.dtype),
                        pltpu.SemaphoreType.DMA((2,))],
        compiler_params=_params(2),
        name="stick_breaking",
    )(q, k_new, v_new, k_past, v_past, k_past, v_past, otd, otp, k_past, v_past)


def _trunk(x, conv_state, k_past, v_past, mem_k, mem_v, w, ff_bf16, *, attn_rows):
    n_seq, seq_len, d = x.shape
    depth, n_a, d_main, alpha = w["depth"], w["n_a"], w["d_main"], w["alpha"]
    xf = x.reshape(n_seq * seq_len, d)
    xb = xf
    if conv_state is None:
        conv_state = jnp.zeros((n_a, n_seq, 2, d_main), F32)
    conv_states = []
    k_new = v_new = None
    for layer in range(depth):
        ff_f32 = None if layer in ff_bf16 else (w["w_ff1"], w["w_ff2"])
        if layer < n_a:
            main, st, q_mem, cast = _inproj_a(xb, conv_state, w["w_in_a"], w["conv_w_t"], layer,
                                              n_seq, seq_len, ff_f32)
            conv_states.append(st)
        else:
            k, v, q, q_mem, cast = _inproj_b(xb, w["w_kv"], w["w_in_b"], layer - n_a, d_main,
                                             n_seq, seq_len, ff_f32, layer)
            if k_new is None:
                k_new, v_new = k, v
            kp = None if k_past is None else jnp.transpose(k_past, (0, 2, 1, 3))
            vp = None if v_past is None else jnp.transpose(v_past, (0, 2, 1, 3))
            main = _stick_breaking(q, k_new, v_new, kp, vp)
        mo = _mem_attn(q_mem, mem_k, mem_v, layer, n_seq, seq_len, rows=attn_rows)
        xf, xb = _out_proj(xf, main, mo, w["w_out"], w["ln_g"], w["ln_b"], layer, alpha)
        if layer not in ff_bf16:
            ff_bf16[layer] = cast or (w["w_ff1"][layer].astype(BF16), w["w_ff2"][layer].astype(BF16))
        xf, xb = _mlp(xb, xf, *ff_bf16[layer], w["ln_g"], w["ln_b"], layer, alpha,
                      want_bf16=layer + 1 < depth)
    return (xf.reshape(n_seq, seq_len, d), jnp.stack(conv_states),
            jnp.transpose(k_new, (0, 2, 1, 3)), jnp.transpose(v_new, (0, 2, 1, 3)))


def kernel(x_prompt, x_sample, state_conv, cache_k, cache_v, cache_mem_k, cache_mem_v, mem_prompt,
           w_in_a, conv_w, w_in_b, w_kv, w_mem_kv, w_out, w_ff1, w_ff2, ln_g, ln_b):
    depth, d_mix, d = w_out.shape
    n_a, d_main = conv_w.shape[0], conv_w.shape[1]
    d_mem = d_mix - d_main
    n_mem_heads = d_mem // HEAD_DIM
    n_prompt, n_mem = mem_prompt.shape[0], mem_prompt.shape[1]
    w = dict(
        depth=depth, n_a=n_a, d_main=d_main, alpha=(2.0 * depth) ** 0.25,
        w_in_a=w_in_a.astype(BF16), conv_w_t=jnp.swapaxes(conv_w, 1, 2),
        w_in_b=w_in_b.astype(BF16), w_kv=w_kv.astype(BF16), w_out=w_out.astype(BF16),
        w_ff1=w_ff1, w_ff2=w_ff2,
        ln_g=ln_g.reshape(depth * 2, 1, d), ln_b=ln_b.reshape(depth * 2, 1, d))

    mem_k_p, mem_v_p = _memkv(mem_prompt.reshape(n_prompt * n_mem, d), w_mem_kv)
    rows_view = lambda a, b: a.reshape(depth, b, n_mem * n_mem_heads, HEAD_DIM)
    mem_k_p, mem_v_p = rows_view(mem_k_p, n_prompt), rows_view(mem_v_p, n_prompt)
    ff_bf16 = {}
    y_p, conv_p, k_p, v_p = _trunk(x_prompt, None, None, None, mem_k_p, mem_v_p, w, ff_bf16,
                                   attn_rows=1024)
    n_sample = x_sample.shape[0]
    y_s, conv_s, k_s, v_s = _trunk(x_sample, state_conv, cache_k, cache_v,
                                   rows_view(cache_mem_k, n_sample), rows_view(cache_mem_v, n_sample),
                                   w, ff_bf16, attn_rows=512)
    mem_shape = (depth, n_prompt, n_mem, n_mem_heads, HEAD_DIM)
    return (y_p, y_s, conv_p, conv_s, k_p, v_p, k_s, v_s,
            mem_k_p.reshape(mem_shape), mem_v_p.reshape(mem_shape))
```

```python
import functools
import math

import jax
import jax.numpy as jnp
import numpy as np
from jax import lax
from jax.experimental import pallas as pl
from jax.experimental.pallas import tpu as pltpu

F32 = jnp.float32
BF16 = jnp.bfloat16

HEAD_DIM = 128
LN_EPS = 1e-5
K_BLOCK = 128
SB_TAIL_ROWS = 32
HALO_ROWS = 8
F32_EXP_UNDERFLOW = 104.0
WEIGHT_TILE = 512
EPILOGUE_ROWS = 128
CONV_CHUNK_ROWS = 256
MLP_EPILOGUE_ROWS = 256
RESIDUAL_FETCH_STEP = 3
VMEM_LIMIT_BYTES = 48 * 1024 * 1024
RIDE_VMEM_LIMIT_BYTES = 56 * 1024 * 1024
MLP_VMEM_LIMIT_BYTES = 58 * 1024 * 1024

_NT = (((1,), (1,)), ((), ()))


def _dot(a, b):
    return jnp.dot(a, b, preferred_element_type=F32)


def _params(grid_rank, vmem_limit_bytes=VMEM_LIMIT_BYTES):
    return pltpu.CompilerParams(dimension_semantics=("arbitrary",) * grid_rank,
                                vmem_limit_bytes=vmem_limit_bytes)


def _layer_norm(y, g, b):
    mu = jnp.mean(y, axis=-1, keepdims=True)
    d = y - mu
    var = jnp.mean(d * d, axis=-1, keepdims=True)
    return d * lax.rsqrt(var + LN_EPS) * g + b


def _seq_tiling(n_seq, seq_len, rows):
    if seq_len >= rows:
        assert seq_len % rows == 0
        return 1, rows, seq_len // rows
    assert rows % seq_len == 0 and n_seq % (rows // seq_len) == 0
    return rows // seq_len, seq_len, 1


def _memkv_kernel(mem_ref, w_ref, k_ref, v_ref, *, n_heads):
    rows = mem_ref.shape[0]
    r = _dot(mem_ref[...].astype(BF16), w_ref[...].astype(BF16))
    for i, ref in enumerate((k_ref, v_ref)):
        for h in range(n_heads):
            col = (i * n_heads + h) * HEAD_DIM
            ref[pl.ds(h, rows, stride=n_heads), :] = r[:, col:col + HEAD_DIM]


def _memkv(mem, w_mem_kv):
    rows, d = mem.shape
    depth, _, two_dmem = w_mem_kv.shape
    n_heads = two_dmem // 2 // HEAD_DIM
    out = jax.ShapeDtypeStruct((depth, rows * n_heads, HEAD_DIM), F32)
    return pl.pallas_call(
        functools.partial(_memkv_kernel, n_heads=n_heads),
        grid=(depth,),
        in_specs=[pl.BlockSpec((rows, d), lambda l: (0, 0)),
                  pl.BlockSpec((None, d, two_dmem), lambda l: (l, 0, 0))],
        out_specs=[pl.BlockSpec((None, rows * n_heads, HEAD_DIM), lambda l: (l, 0, 0))] * 2,
        out_shape=[out, out],
        compiler_params=_params(1),
        name="memkv",
    )(mem, w_mem_kv)


def _ride_along_specs(w_ff1, w_ff2, layer, grid):
    n_steps = grid[0] * grid[1]
    _, d, d_ff = w_ff1.shape
    if d_ff % (n_steps * 128):
        return None
    slab = d_ff // n_steps
    step = lambda i, j: i * grid[1] + j
    in_specs = [pl.BlockSpec((None, d, slab), lambda i, j: (layer, 0, step(i, j))),
                pl.BlockSpec((None, slab, d), lambda i, j: (layer, step(i, j), 0))]
    out_specs = [pl.BlockSpec((d, slab), lambda i, j: (0, step(i, j))),
                 pl.BlockSpec((slab, d), lambda i, j: (step(i, j), 0))]
    out_shape = [jax.ShapeDtypeStruct((d, d_ff), BF16), jax.ShapeDtypeStruct((d_ff, d), BF16)]
    return in_specs, out_specs, out_shape


def _ride_along_cast(rest, n_out, ride):
    if not ride:
        return rest, lambda: None

    def cast():
        rest[2 + n_out][...] = rest[0][...].astype(BF16)
        rest[3 + n_out][...] = rest[1][...].astype(BF16)

    return rest[2:2 + n_out] + rest[4 + n_out:], cast


def _inproj_a_kernel(x_ref, state_ref, wx_ref, wb_ref, wc_ref, cw_ref, wqm_ref, *rest,
                     S, L, tiles_per_seq, ride):
    (main_ref, st_ref, qm_ref, u_scr, carry_scr), cast_slab = _ride_along_cast(rest, 3, ride)
    j = pl.program_id(1)

    @pl.when(j > 0)
    def _():
        cast_slab()
        wx = wx_ref[...]
        wb = wb_ref[...]
        wc = wc_ref[...]
        cw = cw_ref[...]
        w0, w1, w2 = cw[0:1], cw[1:2], cw[2:3]
        prev_rows = slice(HALO_ROWS - 2, HALO_ROWS)
        if tiles_per_seq == 1:
            for s in range(S):
                u_scr[s, prev_rows, :] = state_ref[s]
        else:
            first = (pl.program_id(0) % tiles_per_seq) == 0

            @pl.when(first)
            def _():
                u_scr[0, prev_rows, :] = state_ref[0]

            @pl.when(jnp.logical_not(first))
            def _():
                u_scr[0, prev_rows, :] = carry_scr[j - 1]
        piece = min(CONV_CHUNK_ROWS, L)
        assert CONV_CHUNK_ROWS % piece == 0 and L % piece == 0 and (S * L) % CONV_CHUNK_ROWS == 0
        for r in range(0, S * L, CONV_CHUNK_ROWS):
            x = x_ref[r:r + CONV_CHUNK_ROWS, :].astype(BF16)
            u = _dot(x, wc) * _dot(x, wx)
            gate_b = _dot(x, wb)
            for p in range(0, CONV_CHUNK_ROWS, piece):
                s, t0 = divmod(r + p, L)
                lo = HALO_ROWS + t0
                u_scr[s, lo:lo + piece, :] = u[p:p + piece]
                y = u_scr[s, lo - 2:lo - 2 + piece, :] * w0
                y = y + u_scr[s, lo - 1:lo - 1 + piece, :] * w1
                y = y + u[p:p + piece] * w2
                main_ref[r + p:r + p + piece, :] = (gate_b[p:p + piece] * y).astype(main_ref.dtype)
                if t0 + piece == L:
                    st_ref[s] = u_scr[s, HALO_ROWS + L - 2:HALO_ROWS + L, :]
        if tiles_per_seq > 1:
            carry_scr[j - 1] = u_scr[0, HALO_ROWS + L - 2:HALO_ROWS + L, :]

    @pl.when(j == 0)
    def _():
        cast_slab()
        qm_ref[...] = _dot(x_ref[...].astype(BF16), wqm_ref[...]).astype(BF16)


def _inproj_a(x, state, w_in_a_t, conv_w_t, layer, n_seq, seq_len, ff_weights, *, rows=1024):
    m, d = x.shape
    c = conv_w_t.shape[-1]
    tn = WEIGHT_TILE
    S, L, tps = _seq_tiling(n_seq, seq_len, rows)
    nct = c // tn
    cj = lambda j: jnp.maximum(j - 1, 0)
    grid = (m // rows, nct + 1)
    ride = ff_weights and _ride_along_specs(*ff_weights, layer, grid)
    ride_in, ride_out, ride_shape = ride or ([], [], [])
    kern = functools.partial(_inproj_a_kernel, S=S, L=L, tiles_per_seq=tps, ride=bool(ride))
    w_spec = lambda off: pl.BlockSpec((None, d, tn), lambda i, j: (layer, 0, off + cj(j)))
    main, tile_state, q_mem, *cast = pl.pallas_call(
        kern,
        grid=grid,
        in_specs=[pl.BlockSpec((rows, d), lambda i, j: (i, 0)),
                  pl.BlockSpec((None, S, 2, tn), lambda i, j: (layer, i // tps, 0, cj(j))),
                  w_spec(0), w_spec(nct), w_spec(2 * nct),
                  pl.BlockSpec((None, 3, tn), lambda i, j: (layer, 0, cj(j))),
                  pl.BlockSpec((None, d, tn), lambda i, j: (layer, 0, 3 * nct))] + ride_in,
        out_specs=[pl.BlockSpec((rows, tn), lambda i, j: (i, cj(j))),
                   pl.BlockSpec((S, 2, tn), lambda i, j: (i, 0, cj(j))),
                   pl.BlockSpec((rows, tn), lambda i, j: (i, 0))] + ride_out,
        out_shape=[jax.ShapeDtypeStruct((m, c), BF16),
                   jax.ShapeDtypeStruct((n_seq * tps, 2, c), F32),
                   jax.ShapeDtypeStruct((m, tn), BF16)] + ride_shape,
        scratch_shapes=[pltpu.VMEM((S, HALO_ROWS + L, tn), F32), pltpu.VMEM((nct, 2, tn), F32)],
        compiler_params=_params(2, RIDE_VMEM_LIMIT_BYTES if ride else VMEM_LIMIT_BYTES),
        name="inproj_a",
    )(x, state, w_in_a_t, w_in_a_t, w_in_a_t, conv_w_t, w_in_a_t, *(ff_weights if ride else ()))
    return main, tile_state.reshape(n_seq, tps, 2, c)[:, tps - 1], q_mem, tuple(cast) or None


def _mem_attn_kernel(q_ref, mk_ref, mv_ref, o_ref, *, S, L):
    n_heads = q_ref.shape[1] // HEAD_DIM
    n_mem = mk_ref.shape[1] // n_heads
    scale = 1.0 / math.sqrt(HEAD_DIM)
    tiles = [(slice(s * L, (s + 1) * L), slice(h * HEAD_DIM, (h + 1) * HEAD_DIM),
              s, pl.ds(h, n_mem, stride=n_heads))
             for s in range(S) for h in range(n_heads)]
    sc = jnp.concatenate(
        [lax.dot_general(q_ref[rows, cols], mk_ref[s, head_rows, :].astype(BF16), _NT,
                         preferred_element_type=F32) for rows, cols, s, head_rows in tiles],
        axis=0) * scale
    e = jnp.exp(sc - jnp.max(sc, axis=-1, keepdims=True))
    p = (e / jnp.sum(e, axis=-1, keepdims=True)).astype(BF16)
    for i, (rows, cols, s, head_rows) in enumerate(tiles):
        v = mv_ref[s, head_rows, :].astype(BF16)
        o_ref[rows, cols] = _dot(p[i * L:(i + 1) * L], v).astype(BF16)


def _mem_attn(q_mem, mem_k, mem_v, layer, n_seq, seq_len, *, rows):
    m, d_mem = q_mem.shape
    S, L, tps = _seq_tiling(n_seq, seq_len, rows)
    mem_spec = pl.BlockSpec((None, S) + mem_k.shape[2:], lambda i: (layer, i // tps, 0, 0))
    row_spec = pl.BlockSpec((rows, d_mem), lambda i: (i, 0))
    return pl.pallas_call(
        functools.partial(_mem_attn_kernel, S=S, L=L),
        grid=(m // rows,),
        in_specs=[row_spec, mem_spec, mem_spec],
        out_specs=row_spec,
        out_shape=jax.ShapeDtypeStruct((m, d_mem), BF16),
        compiler_params=_params(1),
        name="mem_attn",
    )(q_mem, mem_k, mem_v)


def _out_proj_kernel(x_ref, main_ref, mo_ref, wo_ref, g_ref, b_ref, yf_ref, yb_ref, *, alpha):
    d_main = main_ref.shape[1]
    for r in range(0, x_ref.shape[0], EPILOGUE_ROWS):
        rows = slice(r, r + EPILOGUE_ROWS)
        mix = (_dot(main_ref[rows, :], wo_ref[0:d_main, :])
               + _dot(mo_ref[rows, :], wo_ref[d_main:, :]))
        y = _layer_norm(alpha * x_ref[rows, :] + mix, g_ref[...], b_ref[...])
        yf_ref[rows, :] = y
        yb_ref[rows, :] = y.astype(BF16)


def _out_proj(xf, main, mo, w_out_b, ln_g, ln_b, layer, alpha, *, rows=512):
    m, d = xf.shape
    d_main, d_mem = main.shape[1], mo.shape[1]
    ln_spec = pl.BlockSpec((None, 1, d), lambda i: (2 * layer, 0, 0))
    return pl.pallas_call(
        functools.partial(_out_proj_kernel, alpha=alpha),
        grid=(m // rows,),
        in_specs=[pl.BlockSpec((rows, d), lambda i: (i, 0)),
                  pl.BlockSpec((rows, d_main), lambda i: (i, 0)),
                  pl.BlockSpec((rows, d_mem), lambda i: (i, 0)),
                  pl.BlockSpec((None, d_main + d_mem, d), lambda i: (layer, 0, 0),
                               pipeline_mode=pl.Buffered(1)),
                  ln_spec, ln_spec],
        out_specs=[pl.BlockSpec((rows, d), lambda i: (i, 0))] * 2,
        out_shape=[jax.ShapeDtypeStruct((m, d), F32), jax.ShapeDtypeStruct((m, d), BF16)],
        compiler_params=_params(1),
        name="out_proj",
    )(xf, main, mo, w_out_b, ln_g, ln_b)


def _mlp_kernel(xb_ref, xf_hbm, w1_ref, w2_ref, g_ref, b_ref, yf_ref, *rest, alpha):
    *yb_ref, xf_buf, xf_sem = rest
    f = pl.program_id(1)
    last = pl.num_programs(1) - 1
    n_rows = yf_ref.shape[0]

    def residual_copy():
        start = pl.multiple_of(pl.program_id(0) * n_rows, n_rows)
        return pltpu.make_async_copy(xf_hbm.at[pl.ds(start, n_rows), :], xf_buf, xf_sem)

    def ff(rows):
        h = jnp.square(jnp.maximum(_dot(xb_ref[rows, :], w1_ref[...]), 0.0)).astype(BF16)
        return _dot(h, w2_ref[...])

    @pl.when(f == 0)
    def _():
        yf_ref[...] = ff(slice(None))

    @pl.when(f == RESIDUAL_FETCH_STEP)
    def _():
        residual_copy().start()

    @pl.when((f > 0) & (f < last))
    def _():
        yf_ref[...] += ff(slice(None))

    @pl.when(f == last)
    def _():
        residual_copy().wait()
        for r in range(0, n_rows, MLP_EPILOGUE_ROWS):
            rows = slice(r, r + MLP_EPILOGUE_ROWS)
            acc = yf_ref[rows, :] + ff(rows)
            y = _layer_norm(alpha * xf_buf[rows, :] + acc, g_ref[...], b_ref[...])
            yf_ref[rows, :] = y
            for ref in yb_ref:
                ref[rows, :] = y.astype(BF16)


def _mlp(xb, xf, w_ff1_b, w_ff2_b, ln_g, ln_b, layer, alpha, *, want_bf16, rows=1024):
    m, d = xf.shape
    tf = WEIGHT_TILE
    n_f = w_ff1_b.shape[1] // tf
    assert n_f > RESIDUAL_FETCH_STEP + 1 and m % rows == 0
    ln_spec = pl.BlockSpec((None, 1, d), lambda i, f: (2 * layer + 1, 0, 0))
    row_spec = pl.BlockSpec((rows, d), lambda i, f: (i, 0))
    out_shape = [jax.ShapeDtypeStruct((m, d), F32)] + [jax.ShapeDtypeStruct((m, d), BF16)] * want_bf16
    out = pl.pallas_call(
        functools.partial(_mlp_kernel, alpha=alpha),
        grid=(m // rows, n_f),
        in_specs=[row_spec,
                  pl.BlockSpec(memory_space=pl.ANY),
                  pl.BlockSpec((d, tf), lambda i, f: (0, f)),
                  pl.BlockSpec((tf, d), lambda i, f: (f, 0)),
                  ln_spec, ln_spec],
        out_specs=[row_spec] * len(out_shape),
        out_shape=out_shape,
        scratch_shapes=[pltpu.VMEM((rows, d), F32), pltpu.SemaphoreType.DMA(())],
        compiler_params=_params(2, MLP_VMEM_LIMIT_BYTES),
        name="mlp",
    )(xb, xf, w_ff1_b, w_ff2_b, ln_g, ln_b)
    return out if want_bf16 else (out[0], None)


def _inproj_b_kernel(x_ref, wk_ref, wv_ref, wq_ref, *rest, S, L, ride):
    (k_ref, v_ref, q_ref, qm_ref), cast_slab = _ride_along_cast(rest, 4, ride)
    j = pl.program_id(1)

    @pl.when(j > 0)
    def _():
        cast_slab()
        x = x_ref[...]
        for w_ref, o_ref in ((wk_ref, k_ref), (wv_ref, v_ref), (wq_ref, q_ref)):
            r = _dot(x, w_ref[...])
            for s in range(S):
                for h in range(o_ref.shape[1]):
                    o_ref[s, h] = r[s * L:(s + 1) * L,
                                    h * HEAD_DIM:(h + 1) * HEAD_DIM].astype(o_ref.dtype)

    @pl.when(j == 0)
    def _():
        cast_slab()
        qm_ref[...] = _dot(x_ref[...], wq_ref[...]).astype(BF16)


def _inproj_b(xb, w_kv_t, w_in_b_t, b_layer, d_main, n_seq, seq_len, ff_weights, ff_layer, *,
              rows=1024):
    m, d = xb.shape
    n_heads = d_main // HEAD_DIM
    tn = WEIGHT_TILE
    nct = d_main // tn
    assert w_in_b_t.shape[-1] == (nct + 1) * tn
    S, L, tps = _seq_tiling(n_seq, seq_len, rows)
    cj = lambda j: jnp.maximum(j - 1, 0)
    out_spec = pl.BlockSpec((S, tn // HEAD_DIM, L, HEAD_DIM),
                            lambda i, j: (i // tps, cj(j), i % tps, 0))
    shape = (n_seq, n_heads, seq_len, HEAD_DIM)
    grid = (m // rows, nct + 1)
    ride = ff_weights and _ride_along_specs(*ff_weights, ff_layer, grid)
    ride_in, ride_out, ride_shape = ride or ([], [], [])
    k, v, q, q_mem, *cast = pl.pallas_call(
        functools.partial(_inproj_b_kernel, S=S, L=L, ride=bool(ride)),
        grid=grid,
        in_specs=[pl.BlockSpec((rows, d), lambda i, j: (i, 0)),
                  pl.BlockSpec((d, tn), lambda i, j: (0, cj(j))),
                  pl.BlockSpec((d, tn), lambda i, j: (0, nct + cj(j))),
                  pl.BlockSpec((None, d, tn),
                               lambda i, j: (b_layer, 0, jnp.where(j == 0, nct, j - 1)))] + ride_in,
        out_specs=[out_spec] * 3 + [pl.BlockSpec((rows, tn), lambda i, j: (i, 0))] + ride_out,
        out_shape=[jax.ShapeDtypeStruct(shape, F32), jax.ShapeDtypeStruct(shape, F32),
                   jax.ShapeDtypeStruct(shape, BF16),
                   jax.ShapeDtypeStruct((m, tn), BF16)] + ride_shape,
        compiler_params=_params(2, RIDE_VMEM_LIMIT_BYTES if ride else VMEM_LIMIT_BYTES),
        name="inproj_b",
    )(xb, w_kv_t, w_kv_t, w_in_b_t, *(ff_weights if ride else ()))
    return k, v, q, q_mem, tuple(cast) or None


def _softplus(z):
    return jnp.maximum(z, 0.0) + jnp.log(1.0 + jnp.exp(-jnp.abs(z)))


def _sb_block(q_ref, k_ref, v_ref, ones_tri, carried_scr, out_scr, *, newest, rows=None):
    n_heads, qb, _ = q_ref.shape
    rows = qb if rows is None else rows
    kb = k_ref.shape[1]
    scale = 1.0 / math.sqrt(HEAD_DIM)
    stack = lambda per_head: jnp.concatenate([per_head(h) for h in range(n_heads)], axis=0)
    z = stack(lambda h: lax.dot_general(q_ref[h, :rows], k_ref[h].astype(BF16), _NT,
                                        preferred_element_type=F32)) * scale
    sp = _softplus(z)
    log_a = z - sp
    if newest:
        row = lax.broadcasted_iota(jnp.int32, z.shape, 0) & (qb - 1)
        col = lax.broadcasted_iota(jnp.int32, z.shape, 1)
        valid = col < row
        sp = jnp.where(valid, sp, 0.0)
    hi = sp.astype(BF16)
    lo = (sp - hi.astype(F32)).astype(BF16)
    if ones_tri.shape[0] == 2 * kb:
        sums = _dot(jnp.concatenate([hi, lo], axis=1), ones_tri)
    else:
        sums = _dot(hi, ones_tri) + _dot(lo, ones_tri)
    log_a = log_a - sums[:, HEAD_DIM:HEAD_DIM + kb]
    if not newest:
        log_a = log_a - stack(lambda h: carried_scr[h, :rows, :kb])
    a = jnp.exp(log_a)
    if newest:
        a = jnp.where(valid, a, 0.0)
    a = a.astype(BF16)
    for h in range(n_heads):
        head = slice(h * rows, (h + 1) * rows)
        contrib = _dot(a[head], v_ref[h].astype(BF16))
        if newest:
            out_scr[h] = contrib
            carried_scr[h] = sums[head, :HEAD_DIM]
        else:
            out_scr[h, :rows] += contrib
            carried_scr[h, :rows] += sums[head, :HEAD_DIM]


def _sb_kernel(q_ref, kd_ref, vd_ref, k1_ref, v1_ref, k2_ref, v2_ref, otd_ref, otp_ref,
               k_hbm, v_hbm, o_ref, carried_scr, out_scr, k_buf, v_buf, sem, *, n_past_static):
    b = pl.program_id(0)
    n_past = pl.program_id(1) if n_past_static is None else n_past_static
    qb = q_ref.shape[1]
    tail = min(SB_TAIL_ROWS, qb)

    def alive():
        mins = [carried_scr[h] for h in range(carried_scr.shape[0])]
        while len(mins) > 1:
            mins = [functools.reduce(jnp.minimum, mins[i:i + 2]) for i in range(0, len(mins), 2)]
        tail_alive = jnp.min(mins[0][tail:]) <= F32_EXP_UNDERFLOW if tail < qb else False
        return tail_alive | (jnp.min(mins[0][:tail]) <= F32_EXP_UNDERFLOW), tail_alive

    def past_block(k_ref, v_ref, rows=None):
        _sb_block(q_ref, k_ref, v_ref, otp_ref[...], carried_scr, out_scr, newest=False, rows=rows)

    _sb_block(q_ref, kd_ref, vd_ref, otd_ref[...], carried_scr, out_scr, newest=True)

    @pl.when(n_past >= 1)
    def _():
        past_block(k1_ref, v1_ref)

    any_alive, tail_alive = alive()

    @pl.when((n_past >= 2) & tail_alive)
    def _():
        past_block(k2_ref, v2_ref)

    @pl.when((n_past >= 2) & any_alive & jnp.logical_not(tail_alive))
    def _():
        past_block(k2_ref, v2_ref, rows=tail)

    n_left = n_past - 2

    def fetch(hbm, buf, slot, blk):
        start = pl.multiple_of(blk * K_BLOCK, K_BLOCK)
        return pltpu.make_async_copy(hbm.at[b, :, pl.ds(start, K_BLOCK), :], buf, sem.at[slot])

    def body(carry):
        i, _ = carry
        blk = n_left - 1 - i
        fetch(k_hbm, k_buf, 0, blk).start()
        fetch(v_hbm, v_buf, 1, blk).start()
        fetch(k_hbm, k_buf, 0, blk).wait()
        fetch(v_hbm, v_buf, 1, blk).wait()
        past_block(k_buf, v_buf)
        return i + 1, alive()[0]

    @pl.when(n_left > 0)
    def _():
        lax.while_loop(lambda c: (c[0] < n_left) & c[1], body, (jnp.int32(0), alive()[0]))

    for h in range(out_scr.shape[0]):
        o_ref[:, h * HEAD_DIM:(h + 1) * HEAD_DIM] = out_scr[h].astype(o_ref.dtype)


def _ones_tri(kb):
    tri = np.tril(np.ones((kb, kb), np.float32), -1)
    m = np.concatenate([np.ones((kb, HEAD_DIM), np.float32), tri], axis=1)
    if kb % 128 == 0:
        m = np.concatenate([m, m], axis=0)
    return jnp.asarray(m, BF16)


def _stick_breaking(q, k_new, v_new, k_past, v_past):
    n_seq, n_heads, seq_len, _ = q.shape
    c = n_heads * HEAD_DIM
    qb = min(K_BLOCK, seq_len)
    assert qb & (qb - 1) == 0
    nq = seq_len // qb
    if k_past is None:
        assert qb == K_BLOCK
        k_past, v_past, past_blocks, n_past_static = k_new, v_new, nq, None
    else:
        assert nq == 1 and k_past.shape[2] % K_BLOCK == 0
        past_blocks = k_past.shape[2] // K_BLOCK
        n_past_static = past_blocks

    def past_spec(back):
        def index_map(b, i):
            n_past = i if n_past_static is None else n_past_static
            return (b, 0, jnp.clip(n_past - back, 0, past_blocks - 1), 0)
        return pl.BlockSpec((None, n_heads, K_BLOCK, HEAD_DIM), index_map)

    new_spec = pl.BlockSpec((None, n_heads, qb, HEAD_DIM), lambda b, i: (b, 0, i, 0))
    hbm_spec = pl.BlockSpec(memory_space=pl.ANY)
    otd, otp = _ones_tri(qb), _ones_tri(K_BLOCK)
    return pl.pallas_call(
        functools.partial(_sb_kernel, n_past_static=n_past_static),
        grid=(n_seq, nq),
        in_specs=[new_spec, new_spec, new_spec,
                  past_spec(1), past_spec(1), past_spec(2), past_spec(2),
                  pl.BlockSpec(otd.shape, lambda b, i: (0, 0)),
                  pl.BlockSpec(otp.shape, lambda b, i: (0, 0)),
                  hbm_spec, hbm_spec],
        out_specs=pl.BlockSpec((qb, c), lambda b, i: (b * nq + i, 0)),
        out_shape=jax.ShapeDtypeStruct((n_seq * seq_len, c), BF16),
        scratch_shapes=[pltpu.VMEM((n_heads, qb, HEAD_DIM), F32),
                        pltpu.VMEM((n_heads, qb, HEAD_DIM), F32),
                        pltpu.VMEM((n_heads, K_BLOCK, HEAD_DIM), k_past.dtype),
                        pltpu.VMEM((n_heads, K_BLOCK, HEAD_DIM), v_past.dtype),
                        pltpu.SemaphoreType.DMA((2,))],
        compiler_params=_params(2),
        name="stick_breaking",
    )(q, k_new, v_new, k_past, v_past, k_past, v_past, otd, otp, k_past, v_past)


def _trunk(x, conv_state, k_past, v_past, mem_k, mem_v, w, ff_bf16, *, attn_rows):
    n_seq, seq_len, d = x.shape
    depth, n_a, d_main, alpha = w["depth"], w["n_a"], w["d_main"], w["alpha"]
    xf = x.reshape(n_seq * seq_len, d)
    xb = xf
    if conv_state is None:
        conv_state = jnp.zeros((n_a, n_seq, 2, d_main), F32)
    conv_states = []
    k_new = v_new = None
    for layer in range(depth):
        ff_f32 = None if layer in ff_bf16 else (w["w_ff1"], w["w_ff2"])
        if layer < n_a:
            main, st, q_mem, cast = _inproj_a(xb, conv_state, w["w_in_a"], w["conv_w_t"], layer,
                                              n_seq, seq_len, ff_f32)
            conv_states.append(st)
        else:
            k, v, q, q_mem, cast = _inproj_b(xb, w["w_kv"], w["w_in_b"], layer - n_a, d_main,
                                             n_seq, seq_len, ff_f32, layer)
            if k_new is None:
                k_new, v_new = k, v
            kp = None if k_past is None else jnp.transpose(k_past, (0, 2, 1, 3))
            vp = None if v_past is None else jnp.transpose(v_past, (0, 2, 1, 3))
            main = _stick_breaking(q, k_new, v_new, kp, vp)
        mo = _mem_attn(q_mem, mem_k, mem_v, layer, n_seq, seq_len, rows=attn_rows)
        xf, xb = _out_proj(xf, main, mo, w["w_out"], w["ln_g"], w["ln_b"], layer, alpha)
        if layer not in ff_bf16:
            ff_bf16[layer] = cast or (w["w_ff1"][layer].astype(BF16), w["w_ff2"][layer].astype(BF16))
        xf, xb = _mlp(xb, xf, *ff_bf16[layer], w["ln_g"], w["ln_b"], layer, alpha,
                      want_bf16=layer + 1 < depth)
    return (xf.reshape(n_seq, seq_len, d), jnp.stack(conv_states),
            jnp.transpose(k_new, (0, 2, 1, 3)), jnp.transpose(v_new, (0, 2, 1, 3)))


def kernel(x_prompt, x_sample, state_conv, cache_k, cache_v, cache_mem_k, cache_mem_v, mem_prompt,
           w_in_a, conv_w, w_in_b, w_kv, w_mem_kv, w_out, w_ff1, w_ff2, ln_g, ln_b):
    depth, d_mix, d = w_out.shape
    n_a, d_main = conv_w.shape[0], conv_w.shape[1]
    d_mem = d_mix - d_main
    n_mem_heads = d_mem // HEAD_DIM
    n_prompt, n_mem = mem_prompt.shape[0], mem_prompt.shape[1]
    w = dict(
        depth=depth, n_a=n_a, d_main=d_main, alpha=(2.0 * depth) ** 0.25,
        w_in_a=w_in_a.astype(BF16), conv_w_t=jnp.swapaxes(conv_w, 1, 2),
        w_in_b=w_in_b.astype(BF16), w_kv=w_kv.astype(BF16), w_out=w_out.astype(BF16),
        w_ff1=w_ff1, w_ff2=w_ff2,
        ln_g=ln_g.reshape(depth * 2, 1, d), ln_b=ln_b.reshape(depth * 2, 1, d))

    mem_k_p, mem_v_p = _memkv(mem_prompt.reshape(n_prompt * n_mem, d), w_mem_kv)
    rows_view = lambda a, b: a.reshape(depth, b, n_mem * n_mem_heads, HEAD_DIM)
    mem_k_p, mem_v_p = rows_view(mem_k_p, n_prompt), rows_view(mem_v_p, n_prompt)
    ff_bf16 = {}
    y_p, conv_p, k_p, v_p = _trunk(x_prompt, None, None, None, mem_k_p, mem_v_p, w, ff_bf16,
                                   attn_rows=1024)
    n_sample = x_sample.shape[0]
    y_s, conv_s, k_s, v_s = _trunk(x_sample, state_conv, cache_k, cache_v,
                                   rows_view(cache_mem_k, n_sample), rows_view(cache_mem_v, n_sample),
                                   w, ff_bf16, attn_rows=512)
    mem_shape = (depth, n_prompt, n_mem, n_mem_heads, HEAD_DIM)
    return (y_p, y_s, conv_p, conv_s, k_p, v_p, k_s, v_s,
            mem_k_p.reshape(mem_shape), mem_v_p.reshape(mem_shape))
```

```python
import functools
import math

import jax
import jax.numpy as jnp
import numpy as np
from jax import lax
from jax.experimental import pallas as pl
from jax.experimental.pallas import tpu as pltpu

F32 = jnp.float32
BF16 = jnp.bfloat16

HEAD_DIM = 128
LN_EPS = 1e-5
K_BLOCK = 128
SB_TAIL_ROWS = 32
HALO_ROWS = 8
F32_EXP_UNDERFLOW = 104.0
WEIGHT_TILE = 512
EPILOGUE_ROWS = 128
CONV_CHUNK_ROWS = 256
MLP_EPILOGUE_ROWS = 256
RESIDUAL_FETCH_STEP = 3
VMEM_LIMIT_BYTES = 48 * 1024 * 1024
RIDE_VMEM_LIMIT_BYTES = 56 * 1024 * 1024
MLP_VMEM_LIMIT_BYTES = 58 * 1024 * 1024

_NT = (((1,), (1,)), ((), ()))


def _dot(a, b):
    return jnp.dot(a, b, preferred_element_type=F32)


def _params(grid_rank, vmem_limit_bytes=VMEM_LIMIT_BYTES):
    return pltpu.CompilerParams(dimension_semantics=("arbitrary",) * grid_rank,
                                vmem_limit_bytes=vmem_limit_bytes)


def _layer_norm(y, g, b):
    mu = jnp.mean(y, axis=-1, keepdims=True)
    d = y - mu
    var = jnp.mean(d * d, axis=-1, keepdims=True)
    return d * lax.rsqrt(var + LN_EPS) * g + b


def _seq_tiling(n_seq, seq_len, rows):
    if seq_len >= rows:
        assert seq_len % rows == 0
        return 1, rows, seq_len // rows
    assert rows % seq_len == 0 and n_seq % (rows // seq_len) == 0
    return rows // seq_len, seq_len, 1


def _memkv_kernel(mem_ref, w_ref, k_ref, v_ref, *, n_heads):
    rows = mem_ref.shape[0]
    r = _dot(mem_ref[...].astype(BF16), w_ref[...].astype(BF16))
    for i, ref in enumerate((k_ref, v_ref)):
        for h in range(n_heads):
            col = (i * n_heads + h) * HEAD_DIM
            ref[pl.ds(h, rows, stride=n_heads), :] = r[:, col:col + HEAD_DIM]


def _memkv(mem, w_mem_kv):
    rows, d = mem.shape
    depth, _, two_dmem = w_mem_kv.shape
    n_heads = two_dmem // 2 // HEAD_DIM
    out = jax.ShapeDtypeStruct((depth, rows * n_heads, HEAD_DIM), F32)
    return pl.pallas_call(
        functools.partial(_memkv_kernel, n_heads=n_heads),
        grid=(depth,),
        in_specs=[pl.BlockSpec((rows, d), lambda l: (0, 0)),
                  pl.BlockSpec((None, d, two_dmem), lambda l: (l, 0, 0))],
        out_specs=[pl.BlockSpec((None, rows * n_heads, HEAD_DIM), lambda l: (l, 0, 0))] * 2,
        out_shape=[out, out],
        compiler_params=_params(1),
        name="memkv",
    )(mem, w_mem_kv)


def _ride_along_specs(w_ff1, w_ff2, layer, grid):
    n_steps = grid[0] * grid[1]
    _, d, d_ff = w_ff1.shape
    if d_ff % (n_steps * 128):
        return None
    slab = d_ff // n_steps
    step = lambda i, j: i * grid[1] + j
    in_specs = [pl.BlockSpec((None, d, slab), lambda i, j: (layer, 0, step(i, j))),
                pl.BlockSpec((None, slab, d), lambda i, j: (layer, step(i, j), 0))]
    out_specs = [pl.BlockSpec((d, slab), lambda i, j: (0, step(i, j))),
                 pl.BlockSpec((slab, d), lambda i, j: (step(i, j), 0))]
    out_shape = [jax.ShapeDtypeStruct((d, d_ff), BF16), jax.ShapeDtypeStruct((d_ff, d), BF16)]
    return in_specs, out_specs, out_shape


def _ride_along_cast(rest, n_out, ride):
    if not ride:
        return rest, lambda: None

    def cast():
        rest[2 + n_out][...] = rest[0][...].astype(BF16)
        rest[3 + n_out][...] = rest[1][...].astype(BF16)

    return rest[2:2 + n_out] + rest[4 + n_out:], cast


def _inproj_a_kernel(x_ref, state_ref, wx_ref, wb_ref, wc_ref, cw_ref, wqm_ref, *rest,
                     S, L, tiles_per_seq, ride):
    (main_ref, st_ref, qm_ref, u_scr, carry_scr), cast_slab = _ride_along_cast(rest, 3, ride)
    j = pl.program_id(1)

    @pl.when(j > 0)
    def _():
        cast_slab()
        wx = wx_ref[...]
        wb = wb_ref[...]
        wc = wc_ref[...]
        cw = cw_ref[...]
        w0, w1, w2 = cw[0:1], cw[1:2], cw[2:3]
        prev_rows = slice(HALO_ROWS - 2, HALO_ROWS)
        if tiles_per_seq == 1:
            for s in range(S):
                u_scr[s, prev_rows, :] = state_ref[s]
        else:
            first = (pl.program_id(0) % tiles_per_seq) == 0

            @pl.when(first)
            def _():
                u_scr[0, prev_rows, :] = state_ref[0]

            @pl.when(jnp.logical_not(first))
            def _():
                u_scr[0, prev_rows, :] = carry_scr[j - 1]
        piece = min(CONV_CHUNK_ROWS, L)
        assert CONV_CHUNK_ROWS % piece == 0 and L % piece == 0 and (S * L) % CONV_CHUNK_ROWS == 0
        for r in range(0, S * L, CONV_CHUNK_ROWS):
            x = x_ref[r:r + CONV_CHUNK_ROWS, :].astype(BF16)
            u = _dot(x, wc) * _dot(x, wx)
            gate_b = _dot(x, wb)
            for p in range(0, CONV_CHUNK_ROWS, piece):
                s, t0 = divmod(r + p, L)
                lo = HALO_ROWS + t0
                u_scr[s, lo:lo + piece, :] = u[p:p + piece]
                y = u_scr[s, lo - 2:lo - 2 + piece, :] * w0
                y = y + u_scr[s, lo - 1:lo - 1 + piece, :] * w1
                y = y + u[p:p + piece] * w2
                main_ref[r + p:r + p + piece, :] = (gate_b[p:p + piece] * y).astype(main_ref.dtype)
                if t0 + piece == L:
                    st_ref[s] = u_scr[s, HALO_ROWS + L - 2:HALO_ROWS + L, :]
        if tiles_per_seq > 1:
            carry_scr[j - 1] = u_scr[0, HALO_ROWS + L - 2:HALO_ROWS + L, :]

    @pl.when(j == 0)
    def _():
        cast_slab()
        qm_ref[...] = _dot(x_ref[...].astype(BF16), wqm_ref[...]).astype(BF16)


def _inproj_a(x, state, w_in_a_t, conv_w_t, layer, n_seq, seq_len, ff_weights, *, rows=1024):
    m, d = x.shape
    c = conv_w_t.shape[-1]
    tn = WEIGHT_TILE
    S, L, tps = _seq_tiling(n_seq, seq_len, rows)
    nct = c // tn
    cj = lambda j: jnp.maximum(j - 1, 0)
    grid = (m // rows, nct + 1)
    ride = ff_weights and _ride_along_specs(*ff_weights, layer, grid)
    ride_in, ride_out, ride_shape = ride or ([], [], [])
    kern = functools.partial(_inproj_a_kernel, S=S, L=L, tiles_per_seq=tps, ride=bool(ride))
    w_spec = lambda off: pl.BlockSpec((None, d, tn), lambda i, j: (layer, 0, off + cj(j)))
    main, tile_state, q_mem, *cast = pl.pallas_call(
        kern,
        grid=grid,
        in_specs=[pl.BlockSpec((rows, d), lambda i, j: (i, 0)),
                  pl.BlockSpec((None, S, 2, tn), lambda i, j: (layer, i // tps, 0, cj(j))),
                  w_spec(0), w_spec(nct), w_spec(2 * nct),
                  pl.BlockSpec((None, 3, tn), lambda i, j: (layer, 0, cj(j))),
                  pl.BlockSpec((None, d, tn), lambda i, j: (layer, 0, 3 * nct))] + ride_in,
        out_specs=[pl.BlockSpec((rows, tn), lambda i, j: (i, cj(j))),
                   pl.BlockSpec((S, 2, tn), lambda i, j: (i, 0, cj(j))),
                   pl.BlockSpec((rows, tn), lambda i, j: (i, 0))] + ride_out,
        out_shape=[jax.ShapeDtypeStruct((m, c), BF16),
                   jax.ShapeDtypeStruct((n_seq * tps, 2, c), F32),
                   jax.ShapeDtypeStruct((m, tn), BF16)] + ride_shape,
        scratch_shapes=[pltpu.VMEM((S, HALO_ROWS + L, tn), F32), pltpu.VMEM((nct, 2, tn), F32)],
        compiler_params=_params(2, RIDE_VMEM_LIMIT_BYTES if ride else VMEM_LIMIT_BYTES),
        name="inproj_a",
    )(x, state, w_in_a_t, w_in_a_t, w_in_a_t, conv_w_t, w_in_a_t, *(ff_weights if ride else ()))
    return main, tile_state.reshape(n_seq, tps, 2, c)[:, tps - 1], q_mem, tuple(cast) or None


def _mem_attention(q_ref, mk_ref, mv_ref, o_ref, S, L):
    n_heads = q_ref.shape[1] // HEAD_DIM
    n_mem = mk_ref.shape[1] // n_heads
    scale = 1.0 / math.sqrt(HEAD_DIM)
    tiles = [(slice(s * L, (s + 1) * L), slice(h * HEAD_DIM, (h + 1) * HEAD_DIM),
              s, pl.ds(h, n_mem, stride=n_heads))
             for s in range(S) for h in range(n_heads)]
    sc = jnp.concatenate(
        [lax.dot_general(q_ref[rows, cols], mk_ref[s, head_rows, :].astype(BF16), _NT,
                         preferred_element_type=F32) for rows, cols, s, head_rows in tiles],
        axis=0) * scale
    e = jnp.exp(sc - jnp.max(sc, axis=-1, keepdims=True))
    p = (e / jnp.sum(e, axis=-1, keepdims=True)).astype(BF16)
    for i, (rows, cols, s, head_rows) in enumerate(tiles):
        v = mv_ref[s, head_rows, :].astype(BF16)
        o_ref[rows, cols] = _dot(p[i * L:(i + 1) * L], v).astype(o_ref.dtype)


def _out_proj_kernel(x_ref, main_ref, q_ref, mk_ref, mv_ref, wo_ref, g_ref, b_ref,
                     yf_ref, yb_ref, mo_scr, *, alpha, S, L):
    d_main = main_ref.shape[1]
    _mem_attention(q_ref, mk_ref, mv_ref, mo_scr, S, L)
    for r in range(0, x_ref.shape[0], EPILOGUE_ROWS):
        rows = slice(r, r + EPILOGUE_ROWS)
        mix = (_dot(main_ref[rows, :], wo_ref[0:d_main, :])
               + _dot(mo_scr[rows, :], wo_ref[d_main:, :]))
        y = _layer_norm(alpha * x_ref[rows, :] + mix, g_ref[...], b_ref[...])
        yf_ref[rows, :] = y
        yb_ref[rows, :] = y.astype(BF16)


def _out_proj(xf, main, q_mem, mem_k, mem_v, w_out_b, ln_g, ln_b, layer, n_seq, seq_len, alpha,
              *, rows=512):
    m, d = xf.shape
    d_main, d_mem = main.shape[1], q_mem.shape[1]
    S, L, tps = _seq_tiling(n_seq, seq_len, rows)
    mem_spec = pl.BlockSpec((None, S) + mem_k.shape[2:], lambda i: (layer, i // tps, 0, 0))
    ln_spec = pl.BlockSpec((None, 1, d), lambda i: (2 * layer, 0, 0))
    return pl.pallas_call(
        functools.partial(_out_proj_kernel, alpha=alpha, S=S, L=L),
        grid=(m // rows,),
        in_specs=[pl.BlockSpec((rows, d), lambda i: (i, 0)),
                  pl.BlockSpec((rows, d_main), lambda i: (i, 0)),
                  pl.BlockSpec((rows, d_mem), lambda i: (i, 0)),
                  mem_spec, mem_spec,
                  pl.BlockSpec((None, d_main + d_mem, d), lambda i: (layer, 0, 0),
                               pipeline_mode=pl.Buffered(1)),
                  ln_spec, ln_spec],
        out_specs=[pl.BlockSpec((rows, d), lambda i: (i, 0))] * 2,
        out_shape=[jax.ShapeDtypeStruct((m, d), F32), jax.ShapeDtypeStruct((m, d), BF16)],
        scratch_shapes=[pltpu.VMEM((rows, d_mem), BF16)],
        compiler_params=_params(1, RIDE_VMEM_LIMIT_BYTES),
        name="out_proj",
    )(xf, main, q_mem, mem_k, mem_v, w_out_b, ln_g, ln_b)


def _mlp_kernel(xb_ref, xf_hbm, w1_ref, w2_ref, g_ref, b_ref, yf_ref, *rest, alpha):
    *yb_ref, xf_buf, xf_sem = rest
    f = pl.program_id(1)
    last = pl.num_programs(1) - 1
    n_rows = yf_ref.shape[0]

    def residual_copy():
        start = pl.multiple_of(pl.program_id(0) * n_rows, n_rows)
        return pltpu.make_async_copy(xf_hbm.at[pl.ds(start, n_rows), :], xf_buf, xf_sem)

    def ff(rows):
        h = jnp.square(jnp.maximum(_dot(xb_ref[rows, :], w1_ref[...]), 0.0)).astype(BF16)
        return _dot(h, w2_ref[...])

    @pl.when(f == 0)
    def _():
        yf_ref[...] = ff(slice(None))

    @pl.when(f == RESIDUAL_FETCH_STEP)
    def _():
        residual_copy().start()

    @pl.when((f > 0) & (f < last))
    def _():
        yf_ref[...] += ff(slice(None))

    @pl.when(f == last)
    def _():
        residual_copy().wait()
        for r in range(0, n_rows, MLP_EPILOGUE_ROWS):
            rows = slice(r, r + MLP_EPILOGUE_ROWS)
            acc = yf_ref[rows, :] + ff(rows)
            y = _layer_norm(alpha * xf_buf[rows, :] + acc, g_ref[...], b_ref[...])
            yf_ref[rows, :] = y
            for ref in yb_ref:
                ref[rows, :] = y.astype(BF16)


def _mlp(xb, xf, w_ff1_b, w_ff2_b, ln_g, ln_b, layer, alpha, *, want_bf16, rows=1024):
    m, d = xf.shape
    tf = WEIGHT_TILE
    n_f = w_ff1_b.shape[1] // tf
    assert n_f > RESIDUAL_FETCH_STEP + 1 and m % rows == 0
    ln_spec = pl.BlockSpec((None, 1, d), lambda i, f: (2 * layer + 1, 0, 0))
    row_spec = pl.BlockSpec((rows, d), lambda i, f: (i, 0))
    out_shape = [jax.ShapeDtypeStruct((m, d), F32)] + [jax.ShapeDtypeStruct((m, d), BF16)] * want_bf16
    out = pl.pallas_call(
        functools.partial(_mlp_kernel, alpha=alpha),
        grid=(m // rows, n_f),
        in_specs=[row_spec,
                  pl.BlockSpec(memory_space=pl.ANY),
                  pl.BlockSpec((d, tf), lambda i, f: (0, f)),
                  pl.BlockSpec((tf, d), lambda i, f: (f, 0)),
                  ln_spec, ln_spec],
        out_specs=[row_spec] * len(out_shape),
        out_shape=out_shape,
        scratch_shapes=[pltpu.VMEM((rows, d), F32), pltpu.SemaphoreType.DMA(())],
        compiler_params=_params(2, MLP_VMEM_LIMIT_BYTES),
        name="mlp",
    )(xb, xf, w_ff1_b, w_ff2_b, ln_g, ln_b)
    return out if want_bf16 else (out[0], None)


def _inproj_b_kernel(x_ref, wk_ref, wv_ref, wq_ref, *rest, S, L, ride):
    (k_ref, v_ref, q_ref, qm_ref), cast_slab = _ride_along_cast(rest, 4, ride)
    j = pl.program_id(1)

    @pl.when(j > 0)
    def _():
        cast_slab()
        x = x_ref[...]
        for w_ref, o_ref in ((wk_ref, k_ref), (wv_ref, v_ref), (wq_ref, q_ref)):
            r = _dot(x, w_ref[...])
            for s in range(S):
                for h in range(o_ref.shape[1]):
                    o_ref[s, h] = r[s * L:(s + 1) * L,
                                    h * HEAD_DIM:(h + 1) * HEAD_DIM].astype(o_ref.dtype)

    @pl.when(j == 0)
    def _():
        cast_slab()
        qm_ref[...] = _dot(x_ref[...], wq_ref[...]).astype(BF16)


def _inproj_b(xb, w_kv_t, w_in_b_t, b_layer, d_main, n_seq, seq_len, ff_weights, ff_layer, *,
              rows=1024):
    m, d = xb.shape
    n_heads = d_main // HEAD_DIM
    tn = WEIGHT_TILE
    nct = d_main // tn
    assert w_in_b_t.shape[-1] == (nct + 1) * tn
    S, L, tps = _seq_tiling(n_seq, seq_len, rows)
    cj = lambda j: jnp.maximum(j - 1, 0)
    out_spec = pl.BlockSpec((S, tn // HEAD_DIM, L, HEAD_DIM),
                            lambda i, j: (i // tps, cj(j), i % tps, 0))
    shape = (n_seq, n_heads, seq_len, HEAD_DIM)
    grid = (m // rows, nct + 1)
    ride = ff_weights and _ride_along_specs(*ff_weights, ff_layer, grid)
    ride_in, ride_out, ride_shape = ride or ([], [], [])
    k, v, q, q_mem, *cast = pl.pallas_call(
        functools.partial(_inproj_b_kernel, S=S, L=L, ride=bool(ride)),
        grid=grid,
        in_specs=[pl.BlockSpec((rows, d), lambda i, j: (i, 0)),
                  pl.BlockSpec((d, tn), lambda i, j: (0, cj(j))),
                  pl.BlockSpec((d, tn), lambda i, j: (0, nct + cj(j))),
                  pl.BlockSpec((None, d, tn),
                               lambda i, j: (b_layer, 0, jnp.where(j == 0, nct, j - 1)))] + ride_in,
        out_specs=[out_spec] * 3 + [pl.BlockSpec((rows, tn), lambda i, j: (i, 0))] + ride_out,
        out_shape=[jax.ShapeDtypeStruct(shape, F32), jax.ShapeDtypeStruct(shape, F32),
                   jax.ShapeDtypeStruct(shape, BF16),
                   jax.ShapeDtypeStruct((m, tn), BF16)] + ride_shape,
        compiler_params=_params(2, RIDE_VMEM_LIMIT_BYTES if ride else VMEM_LIMIT_BYTES),
        name="inproj_b",
    )(xb, w_kv_t, w_kv_t, w_in_b_t, *(ff_weights if ride else ()))
    return k, v, q, q_mem, tuple(cast) or None


def _softplus(z):
    return jnp.maximum(z, 0.0) + jnp.log(1.0 + jnp.exp(-jnp.abs(z)))


def _sb_block(q_ref, k_ref, v_ref, ones_tri, carried_scr, out_scr, *, newest, rows=None):
    n_heads, qb, _ = q_ref.shape
    rows = qb if rows is None else rows
    kb = k_ref.shape[1]
    scale = 1.0 / math.sqrt(HEAD_DIM)
    stack = lambda per_head: jnp.concatenate([per_head(h) for h in range(n_heads)], axis=0)
    z = stack(lambda h: lax.dot_general(q_ref[h, :rows], k_ref[h].astype(BF16), _NT,
                                        preferred_element_type=F32)) * scale
    sp = _softplus(z)
    log_a = z - sp
    if newest:
        row = lax.broadcasted_iota(jnp.int32, z.shape, 0) & (qb - 1)
        col = lax.broadcasted_iota(jnp.int32, z.shape, 1)
        valid = col < row
        sp = jnp.where(valid, sp, 0.0)
    hi = sp.astype(BF16)
    lo = (sp - hi.astype(F32)).astype(BF16)
    if ones_tri.shape[0] == 2 * kb:
        sums = _dot(jnp.concatenate([hi, lo], axis=1), ones_tri)
    else:
        sums = _dot(hi, ones_tri) + _dot(lo, ones_tri)
    log_a = log_a - sums[:, HEAD_DIM:HEAD_DIM + kb]
    if not newest:
        log_a = log_a - stack(lambda h: carried_scr[h, :rows, :kb])
    a = jnp.exp(log_a)
    if newest:
        a = jnp.where(valid, a, 0.0)
    a = a.astype(BF16)
    for h in range(n_heads):
        head = slice(h * rows, (h + 1) * rows)
        contrib = _dot(a[head], v_ref[h].astype(BF16))
        if newest:
            out_scr[h] = contrib
            carried_scr[h] = sums[head, :HEAD_DIM]
        else:
            out_scr[h, :rows] += contrib
            carried_scr[h, :rows] += sums[head, :HEAD_DIM]


def _sb_kernel(q_ref, kd_ref, vd_ref, k1_ref, v1_ref, k2_ref, v2_ref, otd_ref, otp_ref,
               k_hbm, v_hbm, o_ref, carried_scr, out_scr, k_buf, v_buf, sem, *, n_past_static):
    b = pl.program_id(0)
    n_past = pl.program_id(1) if n_past_static is None else n_past_static
    qb = q_ref.shape[1]
    tail = min(SB_TAIL_ROWS, qb)

    def alive():
        mins = [carried_scr[h] for h in range(carried_scr.shape[0])]
        while len(mins) > 1:
            mins = [functools.reduce(jnp.minimum, mins[i:i + 2]) for i in range(0, len(mins), 2)]
        tail_alive = jnp.min(mins[0][tail:]) <= F32_EXP_UNDERFLOW if tail < qb else False
        return tail_alive | (jnp.min(mins[0][:tail]) <= F32_EXP_UNDERFLOW), tail_alive

    def past_block(k_ref, v_ref, rows=None):
        _sb_block(q_ref, k_ref, v_ref, otp_ref[...], carried_scr, out_scr, newest=False, rows=rows)

    _sb_block(q_ref, kd_ref, vd_ref, otd_ref[...], carried_scr, out_scr, newest=True)

    @pl.when(n_past >= 1)
    def _():
        past_block(k1_ref, v1_ref)

    any_alive, tail_alive = alive()

    @pl.when((n_past >= 2) & tail_alive)
    def _():
        past_block(k2_ref, v2_ref)

    @pl.when((n_past >= 2) & any_alive & jnp.logical_not(tail_alive))
    def _():
        past_block(k2_ref, v2_ref, rows=tail)

    n_left = n_past - 2

    def fetch(hbm, buf, slot, blk):
        start = pl.multiple_of(blk * K_BLOCK, K_BLOCK)
        return pltpu.make_async_copy(hbm.at[b, :, pl.ds(start, K_BLOCK), :], buf, sem.at[slot])

    def body(carry):
        i, _ = carry
        blk = n_left - 1 - i
        fetch(k_hbm, k_buf, 0, blk).start()
        fetch(v_hbm, v_buf, 1, blk).start()
        fetch(k_hbm, k_buf, 0, blk).wait()
        fetch(v_hbm, v_buf, 1, blk).wait()
        past_block(k_buf, v_buf)
        return i + 1, alive()[0]

    @pl.when(n_left > 0)
    def _():
        lax.while_loop(lambda c: (c[0] < n_left) & c[1], body, (jnp.int32(0), alive()[0]))

    for h in range(out_scr.shape[0]):
        o_ref[:, h * HEAD_DIM:(h + 1) * HEAD_DIM] = out_scr[h].astype(o_ref.dtype)


def _ones_tri(kb):
    tri = np.tril(np.ones((kb, kb), np.float32), -1)
    m = np.concatenate([np.ones((kb, HEAD_DIM), np.float32), tri], axis=1)
    if kb % 128 == 0:
        m = np.concatenate([m, m], axis=0)
    return jnp.asarray(m, BF16)


def _stick_breaking(q, k_new, v_new, k_past, v_past):
    n_seq, n_heads, seq_len, _ = q.shape
    c = n_heads * HEAD_DIM
    qb = min(K_BLOCK, seq_len)
    assert qb & (qb - 1) == 0
    nq = seq_len // qb
    if k_past is None:
        assert qb == K_BLOCK
        k_past, v_past, past_blocks, n_past_static = k_new, v_new, nq, None
    else:
        assert nq == 1 and k_past.shape[2] % K_BLOCK == 0
        past_blocks = k_past.shape[2] // K_BLOCK
        n_past_static = past_blocks

    def past_spec(back):
        def index_map(b, i):
            n_past = i if n_past_static is None else n_past_static
            return (b, 0, jnp.clip(n_past - back, 0, past_blocks - 1), 0)
        return pl.BlockSpec((None, n_heads, K_BLOCK, HEAD_DIM), index_map)

    new_spec = pl.BlockSpec((None, n_heads, qb, HEAD_DIM), lambda b, i: (b, 0, i, 0))
    hbm_spec = pl.BlockSpec(memory_space=pl.ANY)
    otd, otp = _ones_tri(qb), _ones_tri(K_BLOCK)
    return pl.pallas_call(
        functools.partial(_sb_kernel, n_past_static=n_past_static),
        grid=(n_seq, nq),
        in_specs=[new_spec, new_spec, new_spec,
                  past_spec(1), past_spec(1), past_spec(2), past_spec(2),
                  pl.BlockSpec(otd.shape, lambda b, i: (0, 0)),
                  pl.BlockSpec(otp.shape, lambda b, i: (0, 0)),
                  hbm_spec, hbm_spec],
        out_specs=pl.BlockSpec((qb, c), lambda b, i: (b * nq + i, 0)),
        out_shape=jax.ShapeDtypeStruct((n_seq * seq_len, c), BF16),
        scratch_shapes=[pltpu.VMEM((n_heads, qb, HEAD_DIM), F32),
                        pltpu.VMEM((n_heads, qb, HEAD_DIM), F32),
                        pltpu.VMEM((n_heads, K_BLOCK, HEAD_DIM), k_past.dtype),
                        pltpu.VMEM((n_heads, K_BLOCK, HEAD_DIM), v_past.dtype),
                        pltpu.SemaphoreType.DMA((2,))],
        compiler_params=_params(2),
        name="stick_breaking",
    )(q, k_new, v_new, k_past, v_past, k_past, v_past, otd, otp, k_past, v_past)


def _trunk(x, conv_state, k_past, v_past, mem_k, mem_v, w, ff_bf16, *, attn_rows):
    n_seq, seq_len, d = x.shape
    depth, n_a, d_main, alpha = w["depth"], w["n_a"], w["d_main"], w["alpha"]
    xf = x.reshape(n_seq * seq_len, d)
    xb = xf
    if conv_state is None:
        conv_state = jnp.zeros((n_a, n_seq, 2, d_main), F32)
    conv_states = []
    k_new = v_new = None
    for layer in range(depth):
        ff_f32 = None if layer in ff_bf16 else (w["w_ff1"], w["w_ff2"])
        if layer < n_a:
            main, st, q_mem, cast = _inproj_a(xb, conv_state, w["w_in_a"], w["conv_w_t"], layer,
                                              n_seq, seq_len, ff_f32)
            conv_states.append(st)
        else:
            k, v, q, q_mem, cast = _inproj_b(xb, w["w_kv"], w["w_in_b"], layer - n_a, d_main,
                                             n_seq, seq_len, ff_f32, layer)
            if k_new is None:
                k_new, v_new = k, v
            kp = None if k_past is None else jnp.transpose(k_past, (0, 2, 1, 3))
            vp = None if v_past is None else jnp.transpose(v_past, (0, 2, 1, 3))
            main = _stick_breaking(q, k_new, v_new, kp, vp)
        xf, xb = _out_proj(xf, main, q_mem, mem_k, mem_v, w["w_out"], w["ln_g"], w["ln_b"], layer,
                           n_seq, seq_len, alpha)
        if layer not in ff_bf16:
            ff_bf16[layer] = cast or (w["w_ff1"][layer].astype(BF16), w["w_ff2"][layer].astype(BF16))
        xf, xb = _mlp(xb, xf, *ff_bf16[layer], w["ln_g"], w["ln_b"], layer, alpha,
                      want_bf16=layer + 1 < depth)
    return (xf.reshape(n_seq, seq_len, d), jnp.stack(conv_states),
            jnp.transpose(k_new, (0, 2, 1, 3)), jnp.transpose(v_new, (0, 2, 1, 3)))


def kernel(x_prompt, x_sample, state_conv, cache_k, cache_v, cache_mem_k, cache_mem_v, mem_prompt,
           w_in_a, conv_w, w_in_b, w_kv, w_mem_kv, w_out, w_ff1, w_ff2, ln_g, ln_b):
    depth, d_mix, d = w_out.shape
    n_a, d_main = conv_w.shape[0], conv_w.shape[1]
    d_mem = d_mix - d_main
    n_mem_heads = d_mem // HEAD_DIM
    n_prompt, n_mem = mem_prompt.shape[0], mem_prompt.shape[1]
    w = dict(
        depth=depth, n_a=n_a, d_main=d_main, alpha=(2.0 * depth) ** 0.25,
        w_in_a=w_in_a.astype(BF16), conv_w_t=jnp.swapaxes(conv_w, 1, 2),
        w_in_b=w_in_b.astype(BF16), w_kv=w_kv.astype(BF16), w_out=w_out.astype(BF16),
        w_ff1=w_ff1, w_ff2=w_ff2,
        ln_g=ln_g.reshape(depth * 2, 1, d), ln_b=ln_b.reshape(depth * 2, 1, d))

    mem_k_p, mem_v_p = _memkv(mem_prompt.reshape(n_prompt * n_mem, d), w_mem_kv)
    rows_view = lambda a, b: a.reshape(depth, b, n_mem * n_mem_heads, HEAD_DIM)
    mem_k_p, mem_v_p = rows_view(mem_k_p, n_prompt), rows_view(mem_v_p, n_prompt)
    ff_bf16 = {}
    y_p, conv_p, k_p, v_p = _trunk(x_prompt, None, None, None, mem_k_p, mem_v_p, w, ff_bf16,
                                   attn_rows=1024)
    n_sample = x_sample.shape[0]
    y_s, conv_s, k_s, v_s = _trunk(x_sample, state_conv, cache_k, cache_v,
                                   rows_view(cache_mem_k, n_sample), rows_view(cache_mem_v, n_sample),
                                   w, ff_bf16, attn_rows=512)
    mem_shape = (depth, n_prompt, n_mem, n_mem_heads, HEAD_DIM)
    return (y_p, y_s, conv_p, conv_s, k_p, v_p, k_s, v_s,
            mem_k_p.reshape(mem_shape), mem_v_p.reshape(mem_shape))
```

```python
import functools
import math

import jax
import jax.numpy as jnp
import numpy as np
from jax import lax
from jax.experimental import pallas as pl
from jax.experimental.pallas import tpu as pltpu

F32 = jnp.float32
BF16 = jnp.bfloat16

LANES = 128
HEAD_DIM = 128
LN_EPS = 1e-5
K_BLOCK = 128
SB_TAIL_ROWS = 32
HALO_ROWS = 8
F32_EXP_UNDERFLOW = 104.0
WEIGHT_TILE = 512
EPILOGUE_ROWS = 128
CONV_CHUNK_ROWS = 256
MLP_EPILOGUE_ROWS = 256
RESIDUAL_FETCH_STEP = 3
VMEM_LIMIT_BYTES = 48 * 1024 * 1024
RIDE_VMEM_LIMIT_BYTES = 56 * 1024 * 1024
MLP_VMEM_LIMIT_BYTES = 58 * 1024 * 1024

_NT = (((1,), (1,)), ((), ()))


def _dot(a, b):
    return jnp.dot(a, b, preferred_element_type=F32)


def _params(grid_rank, vmem_limit_bytes=VMEM_LIMIT_BYTES):
    return pltpu.CompilerParams(dimension_semantics=("arbitrary",) * grid_rank,
                                vmem_limit_bytes=vmem_limit_bytes)


def _layer_norm(y, g, b):
    mu = jnp.mean(y, axis=-1, keepdims=True)
    d = y - mu
    var = jnp.mean(d * d, axis=-1, keepdims=True)
    return d * lax.rsqrt(var + LN_EPS) * g + b


def _seq_tiling(n_seq, seq_len, rows):
    if seq_len >= rows:
        assert seq_len % rows == 0
        return 1, rows, seq_len // rows
    assert rows % seq_len == 0 and n_seq % (rows // seq_len) == 0
    return rows // seq_len, seq_len, 1


def _memkv_kernel(mem_ref, w_ref, k_ref, v_ref, *, n_heads):
    rows = mem_ref.shape[0]
    r = _dot(mem_ref[...].astype(BF16), w_ref[...].astype(BF16))
    for i, ref in enumerate((k_ref, v_ref)):
        for h in range(n_heads):
            col = (i * n_heads + h) * HEAD_DIM
            ref[pl.ds(h, rows, stride=n_heads), :] = r[:, col:col + HEAD_DIM]


def _memkv(mem, w_mem_kv):
    rows, d = mem.shape
    depth, _, two_dmem = w_mem_kv.shape
    n_heads = two_dmem // 2 // HEAD_DIM
    out = jax.ShapeDtypeStruct((depth, rows * n_heads, HEAD_DIM), F32)
    return pl.pallas_call(
        functools.partial(_memkv_kernel, n_heads=n_heads),
        grid=(depth,),
        in_specs=[pl.BlockSpec((rows, d), lambda l: (0, 0)),
                  pl.BlockSpec((None, d, two_dmem), lambda l: (l, 0, 0))],
        out_specs=[pl.BlockSpec((None, rows * n_heads, HEAD_DIM), lambda l: (l, 0, 0))] * 2,
        out_shape=[out, out],
        compiler_params=_params(1),
        name="memkv",
    )(mem, w_mem_kv)


def _ride_along_specs(w_ff1, w_ff2, layer, grid):
    n_steps = grid[0] * grid[1]
    _, d, d_ff = w_ff1.shape
    if d_ff % (n_steps * LANES):
        return None
    slab = d_ff // n_steps
    step = lambda i, j: i * grid[1] + j
    in_specs = [pl.BlockSpec((None, d, slab), lambda i, j: (layer, 0, step(i, j))),
                pl.BlockSpec((None, slab, d), lambda i, j: (layer, step(i, j), 0))]
    out_specs = [pl.BlockSpec((d, slab), lambda i, j: (0, step(i, j))),
                 pl.BlockSpec((slab, d), lambda i, j: (step(i, j), 0))]
    out_shape = [jax.ShapeDtypeStruct((d, d_ff), BF16), jax.ShapeDtypeStruct((d_ff, d), BF16)]
    return in_specs, out_specs, out_shape


def _ride_along_cast(rest, n_out, ride):
    if not ride:
        return rest, lambda: None

    def cast():
        rest[2 + n_out][...] = rest[0][...].astype(BF16)
        rest[3 + n_out][...] = rest[1][...].astype(BF16)

    return rest[2:2 + n_out] + rest[4 + n_out:], cast


def _inproj_a_kernel(x_ref, state_ref, wx_ref, wb_ref, wc_ref, cw_ref, wqm_ref, *rest,
                     S, L, tiles_per_seq, ride):
    (main_ref, st_ref, qm_ref, u_scr, carry_scr), cast_slab = _ride_along_cast(rest, 3, ride)
    j = pl.program_id(1)

    @pl.when(j > 0)
    def _():
        cast_slab()
        wx = wx_ref[...]
        wb = wb_ref[...]
        wc = wc_ref[...]
        cw = cw_ref[...]
        w0, w1, w2 = cw[0:1], cw[1:2], cw[2:3]
        prev_rows = slice(HALO_ROWS - 2, HALO_ROWS)
        if tiles_per_seq == 1:
            for s in range(S):
                u_scr[s, prev_rows, :] = state_ref[s]
        else:
            first = (pl.program_id(0) % tiles_per_seq) == 0

            @pl.when(first)
            def _():
                u_scr[0, prev_rows, :] = state_ref[0]

            @pl.when(jnp.logical_not(first))
            def _():
                u_scr[0, prev_rows, :] = carry_scr[j - 1]
        piece = min(CONV_CHUNK_ROWS, L)
        assert CONV_CHUNK_ROWS % piece == 0 and L % piece == 0 and (S * L) % CONV_CHUNK_ROWS == 0
        for r in range(0, S * L, CONV_CHUNK_ROWS):
            x = x_ref[r:r + CONV_CHUNK_ROWS, :].astype(BF16)
            u = _dot(x, wc) * _dot(x, wx)
            gate_b = _dot(x, wb)
            for p in range(0, CONV_CHUNK_ROWS, piece):
                s, t0 = divmod(r + p, L)
                lo = HALO_ROWS + t0
                u_scr[s, lo:lo + piece, :] = u[p:p + piece]
                y = u_scr[s, lo - 2:lo - 2 + piece, :] * w0
                y = y + u_scr[s, lo - 1:lo - 1 + piece, :] * w1
                y = y + u[p:p + piece] * w2
                main_ref[r + p:r + p + piece, :] = (gate_b[p:p + piece] * y).astype(main_ref.dtype)
                if t0 + piece == L:
                    st_ref[s] = u_scr[s, HALO_ROWS + L - 2:HALO_ROWS + L, :]
        if tiles_per_seq > 1:
            carry_scr[j - 1] = u_scr[0, HALO_ROWS + L - 2:HALO_ROWS + L, :]

    @pl.when(j == 0)
    def _():
        cast_slab()
        qm_ref[...] = _dot(x_ref[...].astype(BF16), wqm_ref[...]).astype(BF16)


def _inproj_a(x, state, w_in_a_t, conv_w_t, layer, n_seq, seq_len, ff_weights, *, rows=1024):
    m, d = x.shape
    c = conv_w_t.shape[-1]
    tn = WEIGHT_TILE
    S, L, tps = _seq_tiling(n_seq, seq_len, rows)
    nct = c // tn
    cj = lambda j: jnp.maximum(j - 1, 0)
    grid = (m // rows, nct + 1)
    ride = ff_weights and _ride_along_specs(*ff_weights, layer, grid)
    ride_in, ride_out, ride_shape = ride or ([], [], [])
    kern = functools.partial(_inproj_a_kernel, S=S, L=L, tiles_per_seq=tps, ride=bool(ride))
    w_spec = lambda off: pl.BlockSpec((None, d, tn), lambda i, j: (layer, 0, off + cj(j)))
    main, tile_state, q_mem, *cast = pl.pallas_call(
        kern,
        grid=grid,
        in_specs=[pl.BlockSpec((rows, d), lambda i, j: (i, 0)),
                  pl.BlockSpec((None, S, 2, tn), lambda i, j: (layer, i // tps, 0, cj(j))),
                  w_spec(0), w_spec(nct), w_spec(2 * nct),
                  pl.BlockSpec((None, 3, tn), lambda i, j: (layer, 0, cj(j))),
                  pl.BlockSpec((None, d, tn), lambda i, j: (layer, 0, 3 * nct))] + ride_in,
        out_specs=[pl.BlockSpec((rows, tn), lambda i, j: (i, cj(j))),
                   pl.BlockSpec((S, 2, tn), lambda i, j: (i, 0, cj(j))),
                   pl.BlockSpec((rows, tn), lambda i, j: (i, 0))] + ride_out,
        out_shape=[jax.ShapeDtypeStruct((m, c), BF16),
                   jax.ShapeDtypeStruct((n_seq * tps, 2, c), F32),
                   jax.ShapeDtypeStruct((m, tn), BF16)] + ride_shape,
        scratch_shapes=[pltpu.VMEM((S, HALO_ROWS + L, tn), F32), pltpu.VMEM((nct, 2, tn), F32)],
        compiler_params=_params(2, RIDE_VMEM_LIMIT_BYTES if ride else VMEM_LIMIT_BYTES),
        name="inproj_a",
    )(x, state, w_in_a_t, w_in_a_t, w_in_a_t, conv_w_t, w_in_a_t, *(ff_weights if ride else ()))
    return main, tile_state.reshape(n_seq, tps, 2, c)[:, tps - 1], q_mem, tuple(cast) or None


def _mem_attention(q_ref, mk_ref, mv_ref, o_ref, S, L):
    n_heads = q_ref.shape[1] // HEAD_DIM
    n_mem = mk_ref.shape[1] // n_heads
    scale = 1.0 / math.sqrt(HEAD_DIM)
    tiles = [(slice(s * L, (s + 1) * L), slice(h * HEAD_DIM, (h + 1) * HEAD_DIM),
              s, pl.ds(h, n_mem, stride=n_heads))
             for s in range(S) for h in range(n_heads)]
    sc = jnp.concatenate(
        [lax.dot_general(q_ref[rows, cols], mk_ref[s, head_rows, :].astype(BF16), _NT,
                         preferred_element_type=F32) for rows, cols, s, head_rows in tiles],
        axis=0) * scale
    e = jnp.exp(sc - jnp.max(sc, axis=-1, keepdims=True))
    p = (e / jnp.sum(e, axis=-1, keepdims=True)).astype(BF16)
    for i, (rows, cols, s, head_rows) in enumerate(tiles):
        v = mv_ref[s, head_rows, :].astype(BF16)
        o_ref[rows, cols] = _dot(p[i * L:(i + 1) * L], v).astype(o_ref.dtype)


def _out_proj_kernel(x_ref, main_ref, q_ref, mk_ref, mv_ref, wo_ref, g_ref, b_ref,
                     yf_ref, yb_ref, mo_scr, *, alpha, S, L):
    d_main = main_ref.shape[1]
    _mem_attention(q_ref, mk_ref, mv_ref, mo_scr, S, L)
    for r in range(0, x_ref.shape[0], EPILOGUE_ROWS):
        rows = slice(r, r + EPILOGUE_ROWS)
        mix = (_dot(main_ref[rows, :], wo_ref[0:d_main, :])
               + _dot(mo_scr[rows, :], wo_ref[d_main:, :]))
        y = _layer_norm(alpha * x_ref[rows, :] + mix, g_ref[...], b_ref[...])
        yf_ref[rows, :] = y
        yb_ref[rows, :] = y.astype(BF16)


def _out_proj(xf, main, q_mem, mem_k, mem_v, w_out_b, ln_g, ln_b, layer, n_seq, seq_len, alpha,
              *, rows=512):
    m, d = xf.shape
    d_main, d_mem = main.shape[1], q_mem.shape[1]
    S, L, tps = _seq_tiling(n_seq, seq_len, rows)
    mem_spec = pl.BlockSpec((None, S) + mem_k.shape[2:], lambda i: (layer, i // tps, 0, 0))
    ln_spec = pl.BlockSpec((None, 1, d), lambda i: (2 * layer, 0, 0))
    return pl.pallas_call(
        functools.partial(_out_proj_kernel, alpha=alpha, S=S, L=L),
        grid=(m // rows,),
        in_specs=[pl.BlockSpec((rows, d), lambda i: (i, 0)),
                  pl.BlockSpec((rows, d_main), lambda i: (i, 0)),
                  pl.BlockSpec((rows, d_mem), lambda i: (i, 0)),
                  mem_spec, mem_spec,
                  pl.BlockSpec((None, d_main + d_mem, d), lambda i: (layer, 0, 0),
                               pipeline_mode=pl.Buffered(1)),
                  ln_spec, ln_spec],
        out_specs=[pl.BlockSpec((rows, d), lambda i: (i, 0))] * 2,
        out_shape=[jax.ShapeDtypeStruct((m, d), F32), jax.ShapeDtypeStruct((m, d), BF16)],
        scratch_shapes=[pltpu.VMEM((rows, d_mem), BF16)],
        compiler_params=_params(1, RIDE_VMEM_LIMIT_BYTES),
        name="out_proj",
    )(xf, main, q_mem, mem_k, mem_v, w_out_b, ln_g, ln_b)


def _mlp_kernel(xb_ref, xf_hbm, w1_ref, w2_ref, g_ref, b_ref, yf_ref, *rest, alpha):
    *yb_ref, xf_buf, xf_sem = rest
    f = pl.program_id(1)
    last = pl.num_programs(1) - 1
    n_rows = yf_ref.shape[0]

    def residual_copy():
        start = pl.multiple_of(pl.program_id(0) * n_rows, n_rows)
        return pltpu.make_async_copy(xf_hbm.at[pl.ds(start, n_rows), :], xf_buf, xf_sem)

    def ff(rows):
        h = jnp.square(jnp.maximum(_dot(xb_ref[rows, :], w1_ref[...]), 0.0)).astype(BF16)
        return _dot(h, w2_ref[...])

    @pl.when(f == 0)
    def _():
        yf_ref[...] = ff(slice(None))

    @pl.when(f == RESIDUAL_FETCH_STEP)
    def _():
        residual_copy().start()

    @pl.when((f > 0) & (f < last))
    def _():
        yf_ref[...] += ff(slice(None))

    @pl.when(f == last)
    def _():
        residual_copy().wait()
        for r in range(0, n_rows, MLP_EPILOGUE_ROWS):
            rows = slice(r, r + MLP_EPILOGUE_ROWS)
            acc = yf_ref[rows, :] + ff(rows)
            y = _layer_norm(alpha * xf_buf[rows, :] + acc, g_ref[...], b_ref[...])
            yf_ref[rows, :] = y
            for ref in yb_ref:
                ref[rows, :] = y.astype(BF16)


def _mlp(xb, xf, w_ff1_b, w_ff2_b, ln_g, ln_b, layer, alpha, *, want_bf16, rows=1024):
    m, d = xf.shape
    tf = WEIGHT_TILE
    n_f = w_ff1_b.shape[1] // tf
    assert n_f > RESIDUAL_FETCH_STEP + 1 and m % rows == 0
    ln_spec = pl.BlockSpec((None, 1, d), lambda i, f: (2 * layer + 1, 0, 0))
    row_spec = pl.BlockSpec((rows, d), lambda i, f: (i, 0))
    out_shape = [jax.ShapeDtypeStruct((m, d), F32)] + [jax.ShapeDtypeStruct((m, d), BF16)] * want_bf16
    out = pl.pallas_call(
        functools.partial(_mlp_kernel, alpha=alpha),
        grid=(m // rows, n_f),
        in_specs=[row_spec,
                  pl.BlockSpec(memory_space=pl.ANY),
                  pl.BlockSpec((d, tf), lambda i, f: (0, f)),
                  pl.BlockSpec((tf, d), lambda i, f: (f, 0)),
                  ln_spec, ln_spec],
        out_specs=[row_spec] * len(out_shape),
        out_shape=out_shape,
        scratch_shapes=[pltpu.VMEM((rows, d), F32), pltpu.SemaphoreType.DMA(())],
        compiler_params=_params(2, MLP_VMEM_LIMIT_BYTES),
        name="mlp",
    )(xb, xf, w_ff1_b, w_ff2_b, ln_g, ln_b)
    return out if want_bf16 else (out[0], None)


def _inproj_b_kernel(x_ref, wk_ref, wv_ref, wq_ref, k_ref, v_ref, q_ref, qm_ref, *, S, L):
    j = pl.program_id(1)

    @pl.when(j > 0)
    def _():
        x = x_ref[...]
        for w_ref, o_ref in ((wk_ref, k_ref), (wv_ref, v_ref), (wq_ref, q_ref)):
            r = _dot(x, w_ref[...])
            for s in range(S):
                for h in range(o_ref.shape[1]):
                    o_ref[s, h] = r[s * L:(s + 1) * L,
                                    h * HEAD_DIM:(h + 1) * HEAD_DIM].astype(o_ref.dtype)

    @pl.when(j == 0)
    def _():
        qm_ref[...] = _dot(x_ref[...], wq_ref[...]).astype(BF16)


def _inproj_b(xb, w_kv_t, w_in_b_t, b_layer, d_main, n_seq, seq_len, *, rows=1024):
    m, d = xb.shape
    n_heads = d_main // HEAD_DIM
    tn = WEIGHT_TILE
    nct = d_main // tn
    assert w_in_b_t.shape[-1] == (nct + 1) * tn
    S, L, tps = _seq_tiling(n_seq, seq_len, rows)
    cj = lambda j: jnp.maximum(j - 1, 0)
    out_spec = pl.BlockSpec((S, tn // HEAD_DIM, L, HEAD_DIM),
                            lambda i, j: (i // tps, cj(j), i % tps, 0))
    shape = (n_seq, n_heads, seq_len, HEAD_DIM)
    return pl.pallas_call(
        functools.partial(_inproj_b_kernel, S=S, L=L),
        grid=(m // rows, nct + 1),
        in_specs=[pl.BlockSpec((rows, d), lambda i, j: (i, 0)),
                  pl.BlockSpec((d, tn), lambda i, j: (0, cj(j))),
                  pl.BlockSpec((d, tn), lambda i, j: (0, nct + cj(j))),
                  pl.BlockSpec((None, d, tn),
                               lambda i, j: (b_layer, 0, jnp.where(j == 0, nct, j - 1)))],
        out_specs=[out_spec] * 3 + [pl.BlockSpec((rows, tn), lambda i, j: (i, 0))],
        out_shape=[jax.ShapeDtypeStruct(shape, F32), jax.ShapeDtypeStruct(shape, F32),
                   jax.ShapeDtypeStruct(shape, BF16), jax.ShapeDtypeStruct((m, tn), BF16)],
        compiler_params=_params(2),
        name="inproj_b",
    )(xb, w_kv_t, w_kv_t, w_in_b_t)


def _softplus(z):
    return jnp.maximum(z, 0.0) + jnp.log(1.0 + jnp.exp(-jnp.abs(z)))


def _sb_block(q_ref, k_ref, v_ref, ones_tri, carried_scr, out_scr, *, newest, rows=None):
    n_heads, qb, _ = q_ref.shape
    rows = qb if rows is None else rows
    kb = k_ref.shape[1]
    scale = 1.0 / math.sqrt(HEAD_DIM)
    stack = lambda per_head: jnp.concatenate([per_head(h) for h in range(n_heads)], axis=0)
    z = stack(lambda h: lax.dot_general(q_ref[h, :rows], k_ref[h].astype(BF16), _NT,
                                        preferred_element_type=F32)) * scale
    sp = _softplus(z)
    log_a = z - sp
    if newest:
        row = lax.broadcasted_iota(jnp.int32, z.shape, 0) & (qb - 1)
        col = lax.broadcasted_iota(jnp.int32, z.shape, 1)
        valid = col < row
        sp = jnp.where(valid, sp, 0.0)
    hi = sp.astype(BF16)
    lo = (sp - hi.astype(F32)).astype(BF16)
    if ones_tri.shape[0] == 2 * kb:
        sums = _dot(jnp.concatenate([hi, lo], axis=1), ones_tri)
    else:
        sums = _dot(hi, ones_tri) + _dot(lo, ones_tri)
    log_a = log_a - sums[:, HEAD_DIM:HEAD_DIM + kb]
    if not newest:
        log_a = log_a - stack(lambda h: carried_scr[h, :rows, :kb])
    a = jnp.exp(log_a)
    if newest:
        a = jnp.where(valid, a, 0.0)
    a = a.astype(BF16)
    for h in range(n_heads):
        head = slice(h * rows, (h + 1) * rows)
        contrib = _dot(a[head], v_ref[h].astype(BF16))
        if newest:
            out_scr[h] = contrib
            carried_scr[h] = sums[head, :HEAD_DIM]
        else:
            out_scr[h, :rows] += contrib
            carried_scr[h, :rows] += sums[head, :HEAD_DIM]


def _sb_kernel(q_ref, kd_ref, vd_ref, k1_ref, v1_ref, k2_ref, v2_ref, otd_ref, otp_ref,
               k_hbm, v_hbm, *rest, n_past_static, ride):
    (o_ref, carried_scr, out_scr, k_buf, v_buf, sem), cast_slab = _ride_along_cast(rest, 1, ride)
    cast_slab()
    b = pl.program_id(0)
    n_past = pl.program_id(1) if n_past_static is None else n_past_static
    qb = q_ref.shape[1]
    tail = min(SB_TAIL_ROWS, qb)

    def alive():
        mins = [carried_scr[h] for h in range(carried_scr.shape[0])]
        while len(mins) > 1:
            mins = [functools.reduce(jnp.minimum, mins[i:i + 2]) for i in range(0, len(mins), 2)]
        tail_alive = jnp.min(mins[0][tail:]) <= F32_EXP_UNDERFLOW if tail < qb else False
        return tail_alive | (jnp.min(mins[0][:tail]) <= F32_EXP_UNDERFLOW), tail_alive

    def past_block(k_ref, v_ref, rows=None):
        _sb_block(q_ref, k_ref, v_ref, otp_ref[...], carried_scr, out_scr, newest=False, rows=rows)

    _sb_block(q_ref, kd_ref, vd_ref, otd_ref[...], carried_scr, out_scr, newest=True)

    @pl.when(n_past >= 1)
    def _():
        past_block(k1_ref, v1_ref)

    any_alive, tail_alive = alive()

    @pl.when((n_past >= 2) & tail_alive)
    def _():
        past_block(k2_ref, v2_ref)

    @pl.when((n_past >= 2) & any_alive & jnp.logical_not(tail_alive))
    def _():
        past_block(k2_ref, v2_ref, rows=tail)

    n_left = n_past - 2

    def fetch(hbm, buf, slot, blk):
        start = pl.multiple_of(blk * K_BLOCK, K_BLOCK)
        return pltpu.make_async_copy(hbm.at[b, :, pl.ds(start, K_BLOCK), :], buf, sem.at[slot])

    def body(carry):
        i, _ = carry
        blk = n_left - 1 - i
        fetch(k_hbm, k_buf, 0, blk).start()
        fetch(v_hbm, v_buf, 1, blk).start()
        fetch(k_hbm, k_buf, 0, blk).wait()
        fetch(v_hbm, v_buf, 1, blk).wait()
        past_block(k_buf, v_buf)
        return i + 1, alive()[0]

    @pl.when(n_left > 0)
    def _():
        lax.while_loop(lambda c: (c[0] < n_left) & c[1], body, (jnp.int32(0), alive()[0]))

    for h in range(out_scr.shape[0]):
        o_ref[:, h * HEAD_DIM:(h + 1) * HEAD_DIM] = out_scr[h].astype(o_ref.dtype)


def _ones_tri(kb):
    tri = np.tril(np.ones((kb, kb), np.float32), -1)
    m = np.concatenate([np.ones((kb, HEAD_DIM), np.float32), tri], axis=1)
    if kb % LANES == 0:
        m = np.concatenate([m, m], axis=0)
    return jnp.asarray(m, BF16)


def _stick_breaking(q, k_new, v_new, k_past, v_past, ff_weights, ff_layer):
    n_seq, n_heads, seq_len, _ = q.shape
    c = n_heads * HEAD_DIM
    qb = min(K_BLOCK, seq_len)
    assert qb & (qb - 1) == 0
    nq = seq_len // qb
    if k_past is None:
        assert qb == K_BLOCK
        k_past, v_past, past_blocks, n_past_static = k_new, v_new, nq, None
    else:
        assert nq == 1 and k_past.shape[2] % K_BLOCK == 0
        past_blocks = k_past.shape[2] // K_BLOCK
        n_past_static = past_blocks

    def past_spec(back):
        def index_map(b, i):
            n_past = i if n_past_static is None else n_past_static
            return (b, 0, jnp.clip(n_past - back, 0, past_blocks - 1), 0)
        return pl.BlockSpec((None, n_heads, K_BLOCK, HEAD_DIM), index_map)

    new_spec = pl.BlockSpec((None, n_heads, qb, HEAD_DIM), lambda b, i: (b, 0, i, 0))
    hbm_spec = pl.BlockSpec(memory_space=pl.ANY)
    otd, otp = _ones_tri(qb), _ones_tri(K_BLOCK)
    grid = (n_seq, nq)
    ride = ff_weights and _ride_along_specs(*ff_weights, ff_layer, grid)
    ride_in, ride_out, ride_shape = ride or ([], [], [])
    out, *cast = pl.pallas_call(
        functools.partial(_sb_kernel, n_past_static=n_past_static, ride=bool(ride)),
        grid=grid,
        in_specs=[new_spec, new_spec, new_spec,
                  past_spec(1), past_spec(1), past_spec(2), past_spec(2),
                  pl.BlockSpec(otd.shape, lambda b, i: (0, 0)),
                  pl.BlockSpec(otp.shape, lambda b, i: (0, 0)),
                  hbm_spec, hbm_spec] + ride_in,
        out_specs=[pl.BlockSpec((qb, c), lambda b, i: (b * nq + i, 0))] + ride_out,
        out_shape=[jax.ShapeDtypeStruct((n_seq * seq_len, c), BF16)] + ride_shape,
        scratch_shapes=[pltpu.VMEM((n_heads, qb, HEAD_DIM), F32),
                        pltpu.VMEM((n_heads, qb, HEAD_DIM), F32),
                        pltpu.VMEM((n_heads, K_BLOCK, HEAD_DIM), k_past.dtype),
                        pltpu.VMEM((n_heads, K_BLOCK, HEAD_DIM), v_past.dtype),
                        pltpu.SemaphoreType.DMA((2,))],
        compiler_params=_params(2),
        name="stick_breaking",
    )(q, k_new, v_new, k_past, v_past, k_past, v_past, otd, otp, k_past, v_past,
      *(ff_weights if ride else ()))
    return out, tuple(cast) or None


def _trunk(x, conv_state, k_past, v_past, mem_k, mem_v, w, ff_bf16, *, attn_rows):
    n_seq, seq_len, d = x.shape
    depth, n_a, d_main, alpha = w["depth"], w["n_a"], w["d_main"], w["alpha"]
    xf = x.reshape(n_seq * seq_len, d)
    xb = xf
    if conv_state is None:
        conv_state = jnp.zeros((n_a, n_seq, 2, d_main), F32)
    conv_states = []
    k_new = v_new = None
    for layer in range(depth):
        ff_f32 = None if layer in ff_bf16 else (w["w_ff1"], w["w_ff2"])
        if layer < n_a:
            main, st, q_mem, cast = _inproj_a(xb, conv_state, w["w_in_a"], w["conv_w_t"], layer,
                                              n_seq, seq_len, ff_f32)
            conv_states.append(st)
        else:
            k, v, q, q_mem = _inproj_b(xb, w["w_kv"], w["w_in_b"], layer - n_a, d_main,
                                       n_seq, seq_len)
            if k_new is None:
                k_new, v_new = k, v
            kp = None if k_past is None else jnp.transpose(k_past, (0, 2, 1, 3))
            vp = None if v_past is None else jnp.transpose(v_past, (0, 2, 1, 3))
            main, cast = _stick_breaking(q, k_new, v_new, kp, vp, ff_f32, layer)
        xf, xb = _out_proj(xf, main, q_mem, mem_k, mem_v, w["w_out"], w["ln_g"], w["ln_b"], layer,
                           n_seq, seq_len, alpha)
        if layer not in ff_bf16:
            ff_bf16[layer] = cast or (w["w_ff1"][layer].astype(BF16), w["w_ff2"][layer].astype(BF16))
        xf, xb = _mlp(xb, xf, *ff_bf16[layer], w["ln_g"], w["ln_b"], layer, alpha,
                      want_bf16=layer + 1 < depth)
    return (xf.reshape(n_seq, seq_len, d), jnp.stack(conv_states),
            jnp.transpose(k_new, (0, 2, 1, 3)), jnp.transpose(v_new, (0, 2, 1, 3)))


def kernel(x_prompt, x_sample, state_conv, cache_k, cache_v, cache_mem_k, cache_mem_v, mem_prompt,
           w_in_a, conv_w, w_in_b, w_kv, w_mem_kv, w_out, w_ff1, w_ff2, ln_g, ln_b):
    depth, d_mix, d = w_out.shape
    n_a, d_main = conv_w.shape[0], conv_w.shape[1]
    d_mem = d_mix - d_main
    n_mem_heads = d_mem // HEAD_DIM
    n_prompt, n_mem = mem_prompt.shape[0], mem_prompt.shape[1]
    w = dict(
        depth=depth, n_a=n_a, d_main=d_main, alpha=(2.0 * depth) ** 0.25,
        w_in_a=w_in_a.astype(BF16), conv_w_t=jnp.swapaxes(conv_w, 1, 2),
        w_in_b=w_in_b.astype(BF16), w_kv=w_kv.astype(BF16), w_out=w_out.astype(BF16),
        w_ff1=w_ff1, w_ff2=w_ff2,
        ln_g=ln_g.reshape(depth * 2, 1, d), ln_b=ln_b.reshape(depth * 2, 1, d))

    mem_k_p, mem_v_p = _memkv(mem_prompt.reshape(n_prompt * n_mem, d), w_mem_kv)
    rows_view = lambda a, b: a.reshape(depth, b, n_mem * n_mem_heads, HEAD_DIM)
    mem_k_p, mem_v_p = rows_view(mem_k_p, n_prompt), rows_view(mem_v_p, n_prompt)
    ff_bf16 = {}
    y_p, conv_p, k_p, v_p = _trunk(x_prompt, None, None, None, mem_k_p, mem_v_p, w, ff_bf16,
                                   attn_rows=1024)
    n_sample = x_sample.shape[0]
    y_s, conv_s, k_s, v_s = _trunk(x_sample, state_conv, cache_k, cache_v,
                                   rows_view(cache_mem_k, n_sample), rows_view(cache_mem_v, n_sample),
                                   w, ff_bf16, attn_rows=512)
    mem_shape = (depth, n_prompt, n_mem, n_mem_heads, HEAD_DIM)
    return (y_p, y_s, conv_p, conv_s, k_p, v_p, k_s, v_s,
            mem_k_p.reshape(mem_shape), mem_v_p.reshape(mem_shape))
```

```python
import functools
import math

import jax
import jax.numpy as jnp
import numpy as np
from jax import lax
from jax.experimental import pallas as pl
from jax.experimental.pallas import tpu as pltpu

F32 = jnp.float32
BF16 = jnp.bfloat16

LANES = 128
HEAD_DIM = 128
LN_EPS = 1e-5
K_BLOCK = 128
SB_TAIL_ROWS = 32
SB_SEQS_PER_STEP = 2
HALO_ROWS = 8
F32_EXP_UNDERFLOW = 104.0
WEIGHT_TILE = 512
EPILOGUE_ROWS = 128
CONV_CHUNK_ROWS = 256
MLP_EPILOGUE_ROWS = 256
RESIDUAL_FETCH_STEP = 3
VMEM_LIMIT_BYTES = 48 * 1024 * 1024
RIDE_VMEM_LIMIT_BYTES = 56 * 1024 * 1024
MLP_VMEM_LIMIT_BYTES = 58 * 1024 * 1024

_NT = (((1,), (1,)), ((), ()))


def _dot(a, b):
    return jnp.dot(a, b, preferred_element_type=F32)


def _params(grid_rank, vmem_limit_bytes=VMEM_LIMIT_BYTES):
    return pltpu.CompilerParams(dimension_semantics=("arbitrary",) * grid_rank,
                                vmem_limit_bytes=vmem_limit_bytes)


def _layer_norm(y, g, b):
    mu = jnp.mean(y, axis=-1, keepdims=True)
    d = y - mu
    var = jnp.mean(d * d, axis=-1, keepdims=True)
    return d * lax.rsqrt(var + LN_EPS) * g + b


def _seq_tiling(n_seq, seq_len, rows):
    if seq_len >= rows:
        assert seq_len % rows == 0
        return 1, rows, seq_len // rows
    assert rows % seq_len == 0 and n_seq % (rows // seq_len) == 0
    return rows // seq_len, seq_len, 1


def _memkv_kernel(mem_ref, w_ref, k_ref, v_ref, *, n_heads):
    rows = mem_ref.shape[0]
    r = _dot(mem_ref[...].astype(BF16), w_ref[...].astype(BF16))
    for i, ref in enumerate((k_ref, v_ref)):
        for h in range(n_heads):
            col = (i * n_heads + h) * HEAD_DIM
            ref[pl.ds(h, rows, stride=n_heads), :] = r[:, col:col + HEAD_DIM]


def _memkv(mem, w_mem_kv):
    rows, d = mem.shape
    depth, _, two_dmem = w_mem_kv.shape
    n_heads = two_dmem // 2 // HEAD_DIM
    out = jax.ShapeDtypeStruct((depth, rows * n_heads, HEAD_DIM), F32)
    return pl.pallas_call(
        functools.partial(_memkv_kernel, n_heads=n_heads),
        grid=(depth,),
        in_specs=[pl.BlockSpec((rows, d), lambda l: (0, 0)),
                  pl.BlockSpec((None, d, two_dmem), lambda l: (l, 0, 0))],
        out_specs=[pl.BlockSpec((None, rows * n_heads, HEAD_DIM), lambda l: (l, 0, 0))] * 2,
        out_shape=[out, out],
        compiler_params=_params(1),
        name="memkv",
    )(mem, w_mem_kv)


def _ride_along_specs(w_ff1, w_ff2, layer, grid):
    n_steps = grid[0] * grid[1]
    _, d, d_ff = w_ff1.shape
    if d_ff % (n_steps * LANES):
        return None
    slab = d_ff // n_steps
    step = lambda i, j: i * grid[1] + j
    in_specs = [pl.BlockSpec((None, d, slab), lambda i, j: (layer, 0, step(i, j))),
                pl.BlockSpec((None, slab, d), lambda i, j: (layer, step(i, j), 0))]
    out_specs = [pl.BlockSpec((d, slab), lambda i, j: (0, step(i, j))),
                 pl.BlockSpec((slab, d), lambda i, j: (step(i, j), 0))]
    out_shape = [jax.ShapeDtypeStruct((d, d_ff), BF16), jax.ShapeDtypeStruct((d_ff, d), BF16)]
    return in_specs, out_specs, out_shape


def _ride_along_cast(rest, n_out, ride):
    if not ride:
        return rest, lambda: None

    def cast():
        rest[2 + n_out][...] = rest[0][...].astype(BF16)
        rest[3 + n_out][...] = rest[1][...].astype(BF16)

    return rest[2:2 + n_out] + rest[4 + n_out:], cast


def _inproj_a_kernel(x_ref, state_ref, wx_ref, wb_ref, wc_ref, cw_ref, wqm_ref, *rest,
                     S, L, tiles_per_seq, ride):
    (main_ref, st_ref, qm_ref, u_scr, carry_scr), cast_slab = _ride_along_cast(rest, 3, ride)
    j = pl.program_id(1)

    @pl.when(j > 0)
    def _():
        cast_slab()
        wx = wx_ref[...]
        wb = wb_ref[...]
        wc = wc_ref[...]
        cw = cw_ref[...]
        w0, w1, w2 = cw[0:1], cw[1:2], cw[2:3]
        prev_rows = slice(HALO_ROWS - 2, HALO_ROWS)
        if tiles_per_seq == 1:
            for s in range(S):
                u_scr[s, prev_rows, :] = state_ref[s]
        else:
            first = (pl.program_id(0) % tiles_per_seq) == 0

            @pl.when(first)
            def _():
                u_scr[0, prev_rows, :] = state_ref[0]

            @pl.when(jnp.logical_not(first))
            def _():
                u_scr[0, prev_rows, :] = carry_scr[j - 1]
        piece = min(CONV_CHUNK_ROWS, L)
        assert CONV_CHUNK_ROWS % piece == 0 and L % piece == 0 and (S * L) % CONV_CHUNK_ROWS == 0
        for r in range(0, S * L, CONV_CHUNK_ROWS):
            x = x_ref[r:r + CONV_CHUNK_ROWS, :].astype(BF16)
            u = _dot(x, wc) * _dot(x, wx)
            gate_b = _dot(x, wb)
            for p in range(0, CONV_CHUNK_ROWS, piece):
                s, t0 = divmod(r + p, L)
                lo = HALO_ROWS + t0
                u_scr[s, lo:lo + piece, :] = u[p:p + piece]
                y = u_scr[s, lo - 2:lo - 2 + piece, :] * w0
                y = y + u_scr[s, lo - 1:lo - 1 + piece, :] * w1
                y = y + u[p:p + piece] * w2
                main_ref[r + p:r + p + piece, :] = (gate_b[p:p + piece] * y).astype(main_ref.dtype)
                if t0 + piece == L:
                    st_ref[s] = u_scr[s, HALO_ROWS + L - 2:HALO_ROWS + L, :]
        if tiles_per_seq > 1:
            carry_scr[j - 1] = u_scr[0, HALO_ROWS + L - 2:HALO_ROWS + L, :]

    @pl.when(j == 0)
    def _():
        cast_slab()
        qm_ref[...] = _dot(x_ref[...].astype(BF16), wqm_ref[...]).astype(BF16)


def _inproj_a(x, state, w_in_a_t, conv_w_t, layer, n_seq, seq_len, ff_weights, *, rows=1024):
    m, d = x.shape
    c = conv_w_t.shape[-1]
    tn = WEIGHT_TILE
    S, L, tps = _seq_tiling(n_seq, seq_len, rows)
    nct = c // tn
    cj = lambda j: jnp.maximum(j - 1, 0)
    grid = (m // rows, nct + 1)
    ride = ff_weights and _ride_along_specs(*ff_weights, layer, grid)
    ride_in, ride_out, ride_shape = ride or ([], [], [])
    kern = functools.partial(_inproj_a_kernel, S=S, L=L, tiles_per_seq=tps, ride=bool(ride))
    w_spec = lambda off: pl.BlockSpec((None, d, tn), lambda i, j: (layer, 0, off + cj(j)))
    main, tile_state, q_mem, *cast = pl.pallas_call(
        kern,
        grid=grid,
        in_specs=[pl.BlockSpec((rows, d), lambda i, j: (i, 0)),
                  pl.BlockSpec((None, S, 2, tn), lambda i, j: (layer, i // tps, 0, cj(j))),
                  w_spec(0), w_spec(nct), w_spec(2 * nct),
                  pl.BlockSpec((None, 3, tn), lambda i, j: (layer, 0, cj(j))),
                  pl.BlockSpec((None, d, tn), lambda i, j: (layer, 0, 3 * nct))] + ride_in,
        out_specs=[pl.BlockSpec((rows, tn), lambda i, j: (i, cj(j))),
                   pl.BlockSpec((S, 2, tn), lambda i, j: (i, 0, cj(j))),
                   pl.BlockSpec((rows, tn), lambda i, j: (i, 0))] + ride_out,
        out_shape=[jax.ShapeDtypeStruct((m, c), BF16),
                   jax.ShapeDtypeStruct((n_seq * tps, 2, c), F32),
                   jax.ShapeDtypeStruct((m, tn), BF16)] + ride_shape,
        scratch_shapes=[pltpu.VMEM((S, HALO_ROWS + L, tn), F32), pltpu.VMEM((nct, 2, tn), F32)],
        compiler_params=_params(2, RIDE_VMEM_LIMIT_BYTES if ride else VMEM_LIMIT_BYTES),
        name="inproj_a",
    )(x, state, w_in_a_t, w_in_a_t, w_in_a_t, conv_w_t, w_in_a_t, *(ff_weights if ride else ()))
    return main, tile_state.reshape(n_seq, tps, 2, c)[:, tps - 1], q_mem, tuple(cast) or None


def _mem_attention(q_ref, mk_ref, mv_ref, o_ref, S, L):
    n_heads = q_ref.shape[1] // HEAD_DIM
    n_mem = mk_ref.shape[1] // n_heads
    scale = 1.0 / math.sqrt(HEAD_DIM)
    tiles = [(slice(s * L, (s + 1) * L), slice(h * HEAD_DIM, (h + 1) * HEAD_DIM),
              s, pl.ds(h, n_mem, stride=n_heads))
             for s in range(S) for h in range(n_heads)]
    sc = jnp.concatenate(
        [lax.dot_general(q_ref[rows, cols], mk_ref[s, head_rows, :].astype(BF16), _NT,
                         preferred_element_type=F32) for rows, cols, s, head_rows in tiles],
        axis=0) * scale
    e = jnp.exp(sc - jnp.max(sc, axis=-1, keepdims=True))
    p = (e / jnp.sum(e, axis=-1, keepdims=True)).astype(BF16)
    for i, (rows, cols, s, head_rows) in enumerate(tiles):
        v = mv_ref[s, head_rows, :].astype(BF16)
        o_ref[rows, cols] = _dot(p[i * L:(i + 1) * L], v).astype(o_ref.dtype)


def _out_proj_kernel(x_ref, main_ref, q_ref, mk_ref, mv_ref, wo_ref, g_ref, b_ref,
                     yf_ref, yb_ref, mo_scr, *, alpha, S, L):
    d_main = main_ref.shape[1]
    _mem_attention(q_ref, mk_ref, mv_ref, mo_scr, S, L)
    for r in range(0, x_ref.shape[0], EPILOGUE_ROWS):
        rows = slice(r, r + EPILOGUE_ROWS)
        mix = (_dot(main_ref[rows, :], wo_ref[0:d_main, :])
               + _dot(mo_scr[rows, :], wo_ref[d_main:, :]))
        y = _layer_norm(alpha * x_ref[rows, :] + mix, g_ref[...], b_ref[...])
        yf_ref[rows, :] = y
        yb_ref[rows, :] = y.astype(BF16)


def _out_proj(xf, main, q_mem, mem_k, mem_v, w_out_b, ln_g, ln_b, layer, n_seq, seq_len, alpha,
              *, rows=512):
    m, d = xf.shape
    d_main, d_mem = main.shape[1], q_mem.shape[1]
    S, L, tps = _seq_tiling(n_seq, seq_len, rows)
    mem_spec = pl.BlockSpec((None, S) + mem_k.shape[2:], lambda i: (layer, i // tps, 0, 0))
    ln_spec = pl.BlockSpec((None, 1, d), lambda i: (2 * layer, 0, 0))
    return pl.pallas_call(
        functools.partial(_out_proj_kernel, alpha=alpha, S=S, L=L),
        grid=(m // rows,),
        in_specs=[pl.BlockSpec((rows, d), lambda i: (i, 0)),
                  pl.BlockSpec((rows, d_main), lambda i: (i, 0)),
                  pl.BlockSpec((rows, d_mem), lambda i: (i, 0)),
                  mem_spec, mem_spec,
                  pl.BlockSpec((None, d_main + d_mem, d), lambda i: (layer, 0, 0),
                               pipeline_mode=pl.Buffered(1)),
                  ln_spec, ln_spec],
        out_specs=[pl.BlockSpec((rows, d), lambda i: (i, 0))] * 2,
        out_shape=[jax.ShapeDtypeStruct((m, d), F32), jax.ShapeDtypeStruct((m, d), BF16)],
        scratch_shapes=[pltpu.VMEM((rows, d_mem), BF16)],
        compiler_params=_params(1, RIDE_VMEM_LIMIT_BYTES),
        name="out_proj",
    )(xf, main, q_mem, mem_k, mem_v, w_out_b, ln_g, ln_b)


def _mlp_kernel(xb_ref, xf_hbm, w1_ref, w2_ref, g_ref, b_ref, yf_ref, *rest, alpha):
    *yb_ref, xf_buf, xf_sem = rest
    f = pl.program_id(1)
    last = pl.num_programs(1) - 1
    n_rows = yf_ref.shape[0]

    def residual_copy():
        start = pl.multiple_of(pl.program_id(0) * n_rows, n_rows)
        return pltpu.make_async_copy(xf_hbm.at[pl.ds(start, n_rows), :], xf_buf, xf_sem)

    def ff(rows):
        h = jnp.square(jnp.maximum(_dot(xb_ref[rows, :], w1_ref[...]), 0.0)).astype(BF16)
        return _dot(h, w2_ref[...])

    @pl.when(f == 0)
    def _():
        yf_ref[...] = ff(slice(None))

    @pl.when(f == RESIDUAL_FETCH_STEP)
    def _():
        residual_copy().start()

    @pl.when((f > 0) & (f < last))
    def _():
        yf_ref[...] += ff(slice(None))

    @pl.when(f == last)
    def _():
        residual_copy().wait()
        for r in range(0, n_rows, MLP_EPILOGUE_ROWS):
            rows = slice(r, r + MLP_EPILOGUE_ROWS)
            acc = yf_ref[rows, :] + ff(rows)
            y = _layer_norm(alpha * xf_buf[rows, :] + acc, g_ref[...], b_ref[...])
            yf_ref[rows, :] = y
            for ref in yb_ref:
                ref[rows, :] = y.astype(BF16)


def _mlp(xb, xf, w_ff1_b, w_ff2_b, ln_g, ln_b, layer, alpha, *, want_bf16, rows=1024):
    m, d = xf.shape
    tf = WEIGHT_TILE
    n_f = w_ff1_b.shape[1] // tf
    assert n_f > RESIDUAL_FETCH_STEP + 1 and m % rows == 0
    ln_spec = pl.BlockSpec((None, 1, d), lambda i, f: (2 * layer + 1, 0, 0))
    row_spec = pl.BlockSpec((rows, d), lambda i, f: (i, 0))
    out_shape = [jax.ShapeDtypeStruct((m, d), F32)] + [jax.ShapeDtypeStruct((m, d), BF16)] * want_bf16
    out = pl.pallas_call(
        functools.partial(_mlp_kernel, alpha=alpha),
        grid=(m // rows, n_f),
        in_specs=[row_spec,
                  pl.BlockSpec(memory_space=pl.ANY),
                  pl.BlockSpec((d, tf), lambda i, f: (0, f)),
                  pl.BlockSpec((tf, d), lambda i, f: (f, 0)),
                  ln_spec, ln_spec],
        out_specs=[row_spec] * len(out_shape),
        out_shape=out_shape,
        scratch_shapes=[pltpu.VMEM((rows, d), F32), pltpu.SemaphoreType.DMA(())],
        compiler_params=_params(2, MLP_VMEM_LIMIT_BYTES),
        name="mlp",
    )(xb, xf, w_ff1_b, w_ff2_b, ln_g, ln_b)
    return out if want_bf16 else (out[0], None)


def _inproj_b_kernel(x_ref, wk_ref, wv_ref, wq_ref, k_ref, v_ref, q_ref, qm_ref, *, S, L):
    j = pl.program_id(1)

    @pl.when(j > 0)
    def _():
        x = x_ref[...]
        for w_ref, o_ref in ((wk_ref, k_ref), (wv_ref, v_ref), (wq_ref, q_ref)):
            r = _dot(x, w_ref[...])
            for s in range(S):
                for h in range(o_ref.shape[1]):
                    o_ref[s, h] = r[s * L:(s + 1) * L,
                                    h * HEAD_DIM:(h + 1) * HEAD_DIM].astype(o_ref.dtype)

    @pl.when(j == 0)
    def _():
        qm_ref[...] = _dot(x_ref[...], wq_ref[...]).astype(BF16)


def _inproj_b(xb, w_kv_t, w_in_b_t, b_layer, d_main, n_seq, seq_len, *, rows=1024):
    m, d = xb.shape
    n_heads = d_main // HEAD_DIM
    tn = WEIGHT_TILE
    nct = d_main // tn
    assert w_in_b_t.shape[-1] == (nct + 1) * tn
    S, L, tps = _seq_tiling(n_seq, seq_len, rows)
    cj = lambda j: jnp.maximum(j - 1, 0)
    out_spec = pl.BlockSpec((S, tn // HEAD_DIM, L, HEAD_DIM),
                            lambda i, j: (i // tps, cj(j), i % tps, 0))
    shape = (n_seq, n_heads, seq_len, HEAD_DIM)
    return pl.pallas_call(
        functools.partial(_inproj_b_kernel, S=S, L=L),
        grid=(m // rows, nct + 1),
        in_specs=[pl.BlockSpec((rows, d), lambda i, j: (i, 0)),
                  pl.BlockSpec((d, tn), lambda i, j: (0, cj(j))),
                  pl.BlockSpec((d, tn), lambda i, j: (0, nct + cj(j))),
                  pl.BlockSpec((None, d, tn),
                               lambda i, j: (b_layer, 0, jnp.where(j == 0, nct, j - 1)))],
        out_specs=[out_spec] * 3 + [pl.BlockSpec((rows, tn), lambda i, j: (i, 0))],
        out_shape=[jax.ShapeDtypeStruct(shape, F32), jax.ShapeDtypeStruct(shape, F32),
                   jax.ShapeDtypeStruct(shape, BF16), jax.ShapeDtypeStruct((m, tn), BF16)],
        compiler_params=_params(2),
        name="inproj_b",
    )(xb, w_kv_t, w_kv_t, w_in_b_t)


def _softplus(z):
    return jnp.maximum(z, 0.0) + jnp.log(1.0 + jnp.exp(-jnp.abs(z)))


def _sb_block(q_ref, k_ref, v_ref, ones_tri, carried_scr, out_scr, *, newest, rows=None):
    n_seq, n_heads, qb, _ = q_ref.shape
    rows = qb if rows is None else rows
    kb = k_ref.shape[2]
    scale = 1.0 / math.sqrt(HEAD_DIM)
    pairs = [(s, h) for s in range(n_seq) for h in range(n_heads)]
    stack = lambda per_pair: jnp.concatenate(
        [per_pair(i, s, h) for i, (s, h) in enumerate(pairs)], axis=0)
    z = stack(lambda i, s, h: lax.dot_general(
        q_ref[s, h, :rows], k_ref[s, h].astype(BF16), _NT,
        preferred_element_type=F32)) * scale
    sp = _softplus(z)
    log_a = z - sp
    if newest:
        row = lax.broadcasted_iota(jnp.int32, z.shape, 0) & (qb - 1)
        col = lax.broadcasted_iota(jnp.int32, z.shape, 1)
        valid = col < row
        sp = jnp.where(valid, sp, 0.0)
    hi = sp.astype(BF16)
    lo = (sp - hi.astype(F32)).astype(BF16)
    if ones_tri.shape[0] == 2 * kb:
        sums = _dot(jnp.concatenate([hi, lo], axis=1), ones_tri)
    else:
        sums = _dot(hi, ones_tri) + _dot(lo, ones_tri)
    log_a = log_a - sums[:, HEAD_DIM:HEAD_DIM + kb]
    if not newest:
        log_a = log_a - stack(lambda i, s, h: carried_scr[i, :rows, :kb])
    a = jnp.exp(log_a)
    if newest:
        a = jnp.where(valid, a, 0.0)
    a = a.astype(BF16)
    for i, (s, h) in enumerate(pairs):
        head = slice(i * rows, (i + 1) * rows)
        contrib = _dot(a[head], v_ref[s, h].astype(BF16))
        if newest:
            out_scr[i] = contrib
            carried_scr[i] = sums[head, :HEAD_DIM]
        else:
            out_scr[i, :rows] += contrib
            carried_scr[i, :rows] += sums[head, :HEAD_DIM]


def _sb_kernel(q_ref, kd_ref, vd_ref, k1_ref, v1_ref, k2_ref, v2_ref, otd_ref, otp_ref,
               k_hbm, v_hbm, *rest, n_past_static, ride):
    (o_ref, carried_scr, out_scr, k_buf, v_buf, sem), cast_slab = _ride_along_cast(rest, 1, ride)
    cast_slab()
    b = pl.program_id(0)
    n_past = pl.program_id(1) if n_past_static is None else n_past_static
    n_seq, n_heads, qb, _ = q_ref.shape
    tail = min(SB_TAIL_ROWS, qb)

    def alive():
        mins = [carried_scr[h] for h in range(carried_scr.shape[0])]
        while len(mins) > 1:
            mins = [functools.reduce(jnp.minimum, mins[i:i + 2]) for i in range(0, len(mins), 2)]
        tail_alive = jnp.min(mins[0][tail:]) <= F32_EXP_UNDERFLOW if tail < qb else False
        return tail_alive | (jnp.min(mins[0][:tail]) <= F32_EXP_UNDERFLOW), tail_alive

    def past_block(k_ref, v_ref, rows=None):
        _sb_block(q_ref, k_ref, v_ref, otp_ref[...], carried_scr, out_scr, newest=False, rows=rows)

    _sb_block(q_ref, kd_ref, vd_ref, otd_ref[...], carried_scr, out_scr, newest=True)

    @pl.when(n_past >= 1)
    def _():
        past_block(k1_ref, v1_ref)

    any_alive, tail_alive = alive()

    @pl.when((n_past >= 2) & tail_alive)
    def _():
        past_block(k2_ref, v2_ref)

    @pl.when((n_past >= 2) & any_alive & jnp.logical_not(tail_alive))
    def _():
        past_block(k2_ref, v2_ref, rows=tail)

    n_left = n_past - 2

    def fetch(hbm, buf, slot, blk):
        start = pl.multiple_of(blk * K_BLOCK, K_BLOCK)
        return pltpu.make_async_copy(hbm.at[pl.ds(b * n_seq, n_seq), :, pl.ds(start, K_BLOCK), :],
                                     buf, sem.at[slot])

    def body(carry):
        i, _ = carry
        blk = n_left - 1 - i
        fetch(k_hbm, k_buf, 0, blk).start()
        fetch(v_hbm, v_buf, 1, blk).start()
        fetch(k_hbm, k_buf, 0, blk).wait()
        fetch(v_hbm, v_buf, 1, blk).wait()
        past_block(k_buf, v_buf)
        return i + 1, alive()[0]

    @pl.when(n_left > 0)
    def _():
        lax.while_loop(lambda c: (c[0] < n_left) & c[1], body, (jnp.int32(0), alive()[0]))

    for i in range(n_seq * n_heads):
        s, h = divmod(i, n_heads)
        o_ref[s * qb:(s + 1) * qb, h * HEAD_DIM:(h + 1) * HEAD_DIM] = out_scr[i].astype(o_ref.dtype)


def _ones_tri(kb):
    tri = np.tril(np.ones((kb, kb), np.float32), -1)
    m = np.concatenate([np.ones((kb, HEAD_DIM), np.float32), tri], axis=1)
    if kb % LANES == 0:
        m = np.concatenate([m, m], axis=0)
    return jnp.asarray(m, BF16)


def _stick_breaking(q, k_new, v_new, k_past, v_past, ff_weights, ff_layer):
    n_seq, n_heads, seq_len, _ = q.shape
    c = n_heads * HEAD_DIM
    qb = min(K_BLOCK, seq_len)
    assert qb & (qb - 1) == 0
    nq = seq_len // qb
    if k_past is None:
        assert qb == K_BLOCK
        k_past, v_past, past_blocks, n_past_static = k_new, v_new, nq, None
        S = 1
    else:
        assert nq == 1 and k_past.shape[2] % K_BLOCK == 0
        past_blocks = k_past.shape[2] // K_BLOCK
        n_past_static = past_blocks
        S = SB_SEQS_PER_STEP if n_seq % SB_SEQS_PER_STEP == 0 else 1

    def past_spec(back):
        def index_map(b, i):
            n_past = i if n_past_static is None else n_past_static
            return (b, 0, jnp.clip(n_past - back, 0, past_blocks - 1), 0)
        return pl.BlockSpec((S, n_heads, K_BLOCK, HEAD_DIM), index_map)

    new_spec = pl.BlockSpec((S, n_heads, qb, HEAD_DIM), lambda b, i: (b, 0, i, 0))
    hbm_spec = pl.BlockSpec(memory_space=pl.ANY)
    otd, otp = _ones_tri(qb), _ones_tri(K_BLOCK)
    grid = (n_seq // S, nq)
    ride = ff_weights and _ride_along_specs(*ff_weights, ff_layer, grid)
    ride_in, ride_out, ride_shape = ride or ([], [], [])
    out, *cast = pl.pallas_call(
        functools.partial(_sb_kernel, n_past_static=n_past_static, ride=bool(ride)),
        grid=grid,
        in_specs=[new_spec, new_spec, new_spec,
                  past_spec(1), past_spec(1), past_spec(2), past_spec(2),
                  pl.BlockSpec(otd.shape, lambda b, i: (0, 0)),
                  pl.BlockSpec(otp.shape, lambda b, i: (0, 0)),
                  hbm_spec, hbm_spec] + ride_in,
        out_specs=[pl.BlockSpec((S * qb, c), lambda b, i: (b * nq + i, 0))] + ride_out,
        out_shape=[jax.ShapeDtypeStruct((n_seq * seq_len, c), BF16)] + ride_shape,
        scratch_shapes=[pltpu.VMEM((S * n_heads, qb, HEAD_DIM), F32),
                        pltpu.VMEM((S * n_heads, qb, HEAD_DIM), F32),
                        pltpu.VMEM((S, n_heads, K_BLOCK, HEAD_DIM), k_past.dtype),
                        pltpu.VMEM((S, n_heads, K_BLOCK, HEAD_DIM), v_past.dtype),
                        pltpu.SemaphoreType.DMA((2,))],
        compiler_params=_params(2),
        name="stick_breaking",
    )(q, k_new, v_new, k_past, v_past, k_past, v_past, otd, otp, k_past, v_past,
      *(ff_weights if ride else ()))
    return out, tuple(cast) or None


def _trunk(x, conv_state, k_past, v_past, mem_k, mem_v, w, ff_bf16, *, attn_rows):
    n_seq, seq_len, d = x.shape
    depth, n_a, d_main, alpha = w["depth"], w["n_a"], w["d_main"], w["alpha"]
    xf = x.reshape(n_seq * seq_len, d)
    xb = xf
    if conv_state is None:
        conv_state = jnp.zeros((n_a, n_seq, 2, d_main), F32)
    conv_states = []
    k_new = v_new = None
    for layer in range(depth):
        ff_f32 = None if layer in ff_bf16 else (w["w_ff1"], w["w_ff2"])
        if layer < n_a:
            main, st, q_mem, cast = _inproj_a(xb, conv_state, w["w_in_a"], w["conv_w_t"], layer,
                                              n_seq, seq_len, ff_f32)
            conv_states.append(st)
        else:
            k, v, q, q_mem = _inproj_b(xb, w["w_kv"], w["w_in_b"], layer - n_a, d_main,
                                       n_seq, seq_len)
            if k_new is None:
                k_new, v_new = k, v
            kp = None if k_past is None else jnp.transpose(k_past, (0, 2, 1, 3))
            vp = None if v_past is None else jnp.transpose(v_past, (0, 2, 1, 3))
            main, cast = _stick_breaking(q, k_new, v_new, kp, vp, ff_f32, layer)
        xf, xb = _out_proj(xf, main, q_mem, mem_k, mem_v, w["w_out"], w["ln_g"], w["ln_b"], layer,
                           n_seq, seq_len, alpha)
        if layer not in ff_bf16:
            ff_bf16[layer] = cast or (w["w_ff1"][layer].astype(BF16), w["w_ff2"][layer].astype(BF16))
        xf, xb = _mlp(xb, xf, *ff_bf16[layer], w["ln_g"], w["ln_b"], layer, alpha,
                      want_bf16=layer + 1 < depth)
    return (xf.reshape(n_seq, seq_len, d), jnp.stack(conv_states),
            jnp.transpose(k_new, (0, 2, 1, 3)), jnp.transpose(v_new, (0, 2, 1, 3)))


def kernel(x_prompt, x_sample, state_conv, cache_k, cache_v, cache_mem_k, cache_mem_v, mem_prompt,
           w_in_a, conv_w, w_in_b, w_kv, w_mem_kv, w_out, w_ff1, w_ff2, ln_g, ln_b):
    depth, d_mix, d = w_out.shape
    n_a, d_main = conv_w.shape[0], conv_w.shape[1]
    d_mem = d_mix - d_main
    n_mem_heads = d_mem // HEAD_DIM
    n_prompt, n_mem = mem_prompt.shape[0], mem_prompt.shape[1]
    w = dict(
        depth=depth, n_a=n_a, d_main=d_main, alpha=(2.0 * depth) ** 0.25,
        w_in_a=w_in_a.astype(BF16), conv_w_t=jnp.swapaxes(conv_w, 1, 2),
        w_in_b=w_in_b.astype(BF16), w_kv=w_kv.astype(BF16), w_out=w_out.astype(BF16),
        w_ff1=w_ff1, w_ff2=w_ff2,
        ln_g=ln_g.reshape(depth * 2, 1, d), ln_b=ln_b.reshape(depth * 2, 1, d))

    mem_k_p, mem_v_p = _memkv(mem_prompt.reshape(n_prompt * n_mem, d), w_mem_kv)
    rows_view = lambda a, b: a.reshape(depth, b, n_mem * n_mem_heads, HEAD_DIM)
    mem_k_p, mem_v_p = rows_view(mem_k_p, n_prompt), rows_view(mem_v_p, n_prompt)
    ff_bf16 = {}
    y_p, conv_p, k_p, v_p = _trunk(x_prompt, None, None, None, mem_k_p, mem_v_p, w, ff_bf16,
                                   attn_rows=1024)
    n_sample = x_sample.shape[0]
    y_s, conv_s, k_s, v_s = _trunk(x_sample, state_conv, cache_k, cache_v,
                                   rows_view(cache_mem_k, n_sample), rows_view(cache_mem_v, n_sample),
                                   w, ff_bf16, attn_rows=512)
    mem_shape = (depth, n_prompt, n_mem, n_mem_heads, HEAD_DIM)
    return (y_p, y_s, conv_p, conv_s, k_p, v_p, k_s, v_s,
            mem_k_p.reshape(mem_shape), mem_v_p.reshape(mem_shape))
```
